```python
import math
import jax
import jax.numpy as jnp
from jax import lax
import numpy as np


D_MODEL = 1024
BATCH = 2
SEQ = 8192
DEPTH = 4

CHUNK = 64
N_MIXERS = 3
N_A = (DEPTH + 2) // 3
N_B = (DEPTH + 1) // 3
N_C = DEPTH // 3
D_FF = 2816
EPS = 1e-6
NEG_INF = -1e30
HEADS_A = 16
HEAD_DIM_A = D_MODEL // HEADS_A
LEFT_CHUNKS = 8
BAND = (LEFT_CHUNKS + 1) * CHUNK
REL_CLIP = 128
GMLP_CHUNK = 128
D_GATE = D_MODEL
GMLP_GROUPS = 8
GMLP_GROUP_DIM = D_GATE // GMLP_GROUPS
HEADS_C = 8
HEAD_DIM_C = D_MODEL // (2 * HEADS_C)
Q_BLOCK = 128

kernel_name = 'hybrid_streaming_interleaved_block'


def rms_norm(x, g):
    xf = x.astype(jnp.float32)
    y = xf * lax.rsqrt(jnp.mean(xf * xf, axis=-1, keepdims=True) + EPS)
    return (y * g.astype(jnp.float32)).astype(x.dtype)


def swiglu(h, w_in, w_out):
    gate, up = jnp.split(h @ w_in, 2, axis=-1)
    return (jax.nn.silu(gate) * up) @ w_out


def mixer_a(h, w_qkv, rel_bias, w_o):
    B, S, D = h.shape
    nc = S // CHUNK
    pad = LEFT_CHUNKS * CHUNK
    q, k, v = jnp.split(h @ w_qkv, 3, axis=-1)
    q = q.reshape(B, nc, CHUNK, HEADS_A, HEAD_DIM_A).transpose(1, 0, 2, 3, 4)
    k = jnp.pad(k.reshape(B, S, HEADS_A, HEAD_DIM_A), ((0, 0), (pad, 0), (0, 0), (0, 0)))
    v = jnp.pad(v.reshape(B, S, HEADS_A, HEAD_DIM_A), ((0, 0), (pad, 0), (0, 0), (0, 0)))
    rel = jnp.clip(pad + jnp.arange(CHUNK)[:, None] - jnp.arange(BAND)[None, :], -REL_CLIP, REL_CLIP) + REL_CLIP
    bias = rel_bias.astype(jnp.float32)[:, rel]
    scale = HEAD_DIM_A ** -0.5

    def one_chunk(args):
        c, qc = args
        start = c * CHUNK
        kb = lax.dynamic_slice_in_dim(k, start, BAND, axis=1)
        vb = lax.dynamic_slice_in_dim(v, start, BAND, axis=1)
        s = jnp.einsum('bqhd,bkhd->bhqk', qc, kb).astype(jnp.float32) * scale + bias
        valid = (start - pad + jnp.arange(BAND)) >= 0
        s = jnp.where(valid, s, NEG_INF)
        p = jax.nn.softmax(s, axis=-1).astype(vb.dtype)
        return jnp.einsum('bhqk,bkhd->bqhd', p, vb)

    o = lax.map(one_chunk, (jnp.arange(nc), q))
    o = o.transpose(1, 0, 2, 3, 4).reshape(B, S, D)
    return o @ w_o


def mixer_b(h, w_in, ln_g, ln_b, w_s, b_s, w_o):
    B, S, _ = h.shape
    n = S // GMLP_CHUNK
    u, v = jnp.split(jax.nn.gelu(h @ w_in, approximate=False), 2, axis=-1)
    vf = v.astype(jnp.float32)
    mu = jnp.mean(vf, axis=-1, keepdims=True)
    var = jnp.mean(jnp.square(vf - mu), axis=-1, keepdims=True)
    v = ((vf - mu) * lax.rsqrt(var + EPS) * ln_g.astype(jnp.float32) + ln_b.astype(jnp.float32)).astype(h.dtype)
    v = v.reshape(B, n, GMLP_CHUNK, GMLP_GROUPS, GMLP_GROUP_DIM)
    causal = jnp.tril(jnp.ones((GMLP_CHUNK, GMLP_CHUNK), dtype=bool))
    w = jnp.where(causal[None], w_s, 0.0)
    sv = jnp.einsum('gts,bnsgc->bntgc', w, v) + b_s.T[None, None, :, :, None]
    y = u * sv.reshape(B, S, D_GATE)
    return y @ w_o


def mixer_c(h, w_qkv, lam, subln_g, w_o, lambda_init):
    B, S, D = h.shape
    nb = S // Q_BLOCK
    q, k, v = jnp.split(h @ w_qkv, 3, axis=-1)
    q = q.reshape(B, nb, Q_BLOCK, HEADS_C, 2, HEAD_DIM_C).transpose(1, 0, 2, 3, 4, 5)
    k = k.reshape(B, S, HEADS_C, 2, HEAD_DIM_C)
    v = v.reshape(B, S, HEADS_C, 2 * HEAD_DIM_C)
    lamf = lam.astype(jnp.float32)
    lam_full = jnp.exp(jnp.sum(lamf[0] * lamf[1])) - jnp.exp(jnp.sum(lamf[2] * lamf[3])) + lambda_init
    slopes = 2.0 ** (-8.0 * (jnp.arange(HEADS_C, dtype=jnp.float32) + 1.0) / HEADS_C)
    kpos = jnp.arange(S)
    scale = HEAD_DIM_C ** -0.5

    def one_block(args):
        blk, qblk = args
        qpos = blk * Q_BLOCK + jnp.arange(Q_BLOCK)
        dist = jnp.abs(qpos[:, None] - kpos[None, :]).astype(jnp.float32)
        alibi = -slopes[:, None, None] * dist
        allowed = kpos[None, :] < (qpos[:, None] // CHUNK + 1) * CHUNK
        s = jnp.einsum('bqhmd,bkhmd->bmhqk', qblk, k).astype(jnp.float32) * scale + alibi
        s = jnp.where(allowed, s, NEG_INF)
        p = jax.nn.softmax(s, axis=-1)
        a = p[:, 0] - lam_full * p[:, 1]
        return jnp.einsum('bhqk,bkhe->bqhe', a.astype(v.dtype), v)

    o = lax.map(one_block, (jnp.arange(nb), q))
    o = o.transpose(1, 0, 2, 3, 4).reshape(B, S, HEADS_C, 2 * HEAD_DIM_C)
    o = rms_norm(o, subln_g) * (1.0 - lambda_init)
    return o.reshape(B, S, D) @ w_o


def setup_inputs(seed: int = 0) -> dict:
    key = jax.random.key(seed)
    ks = jax.random.split(key, 20)

    def nrm(k, shape, scale):
        return jax.random.normal(k, shape, jnp.float32) * scale

    D = D_MODEL
    return {
        'x': nrm(ks[0], (BATCH, SEQ, D), 1.0),
        'norm_g': 1.0 + nrm(ks[1], (DEPTH, 6, D), 0.05),
        'ff1_w_in': nrm(ks[2], (DEPTH, D, 2 * D_FF), D ** -0.5),
        'ff1_w_out': nrm(ks[3], (DEPTH, D_FF, D), D_FF ** -0.5),
        'ff2_w_in': nrm(ks[4], (DEPTH, D, 2 * D_FF), D ** -0.5),
        'ff2_w_out': nrm(ks[5], (DEPTH, D_FF, D), D_FF ** -0.5),
        'a_w_qkv': nrm(ks[6], (N_A, D, 3 * D), D ** -0.5),
        'a_rel_bias': nrm(ks[7], (N_A, HEADS_A, 2 * REL_CLIP + 1), 0.5),
        'a_w_o': nrm(ks[8], (N_A, D, D), D ** -0.5),
        'b_w_in': nrm(ks[9], (N_B, D, 2 * D_GATE), D ** -0.5),
        'b_ln_g': 1.0 + nrm(ks[10], (N_B, D_GATE), 0.05),
        'b_ln_b': nrm(ks[11], (N_B, D_GATE), 0.05),
        'b_w_s': nrm(ks[12], (N_B, GMLP_GROUPS, GMLP_CHUNK, GMLP_CHUNK), GMLP_CHUNK ** -0.5),
        'b_b_s': 1.0 + nrm(ks[13], (N_B, GMLP_GROUPS, GMLP_CHUNK), 0.1),
        'b_w_o': nrm(ks[14], (N_B, D_GATE, D), D_GATE ** -0.5),
        'c_w_qkv': nrm(ks[15], (N_C, D, 3 * D), D ** -0.5),
        'c_lambda': nrm(ks[16], (N_C, 4, HEAD_DIM_C), 0.1),
        'c_subln_g': 1.0 + nrm(ks[17], (N_C, 2 * HEAD_DIM_C), 0.05),
        'c_w_o': nrm(ks[18], (N_C, D, D), D ** -0.5),
    }


def reference(x, norm_g, ff1_w_in, ff1_w_out, ff2_w_in, ff2_w_out, a_w_qkv, a_rel_bias, a_w_o, b_w_in, b_ln_g, b_ln_b, b_w_s, b_b_s, b_w_o, c_w_qkv, c_lambda, c_subln_g, c_w_o):
    for i in range(DEPTH):
        g = norm_g[i]
        x = x + 0.5 * rms_norm(swiglu(rms_norm(x, g[0]), ff1_w_in[i], ff1_w_out[i]), g[1])
        h = rms_norm(x, g[2])
        kind, j = i % N_MIXERS, i // N_MIXERS
        if kind == 0:
            m = mixer_a(h, a_w_qkv[j], a_rel_bias[j], a_w_o[j])
        elif kind == 1:
            m = mixer_b(h, b_w_in[j], b_ln_g[j], b_ln_b[j], b_w_s[j], b_b_s[j], b_w_o[j])
        else:
            lambda_init = 0.8 - 0.6 * math.exp(-0.3 * i)
            m = mixer_c(h, c_w_qkv[j], c_lambda[j], c_subln_g[j], c_w_o[j], lambda_init)
        x = x + rms_norm(m, g[3])
        x = x + 0.5 * rms_norm(swiglu(rms_norm(x, g[4]), ff2_w_in[i], ff2_w_out[i]), g[5])
    return x
```

```python
import functools
import math

import jax
import jax.numpy as jnp
import numpy as np
from jax import lax
from jax.experimental import pallas as pl
from jax.experimental.pallas import tpu as pltpu

F32 = jnp.float32
BF16 = jnp.bfloat16

D_MODEL = 1024
D_FF = 2816
EPS = 1e-6
NEG_INF = -1e30
N_MIXERS = 3

CHUNK = 64
LEFT = 8 * CHUNK
HEADS_A = 16
REL_CLIP = 128
GMLP_CHUNK = 128
GMLP_GROUPS = 8
HEADS_C = 8
HEAD_W = 128

V7X_VMEM_LIMIT_BYTES = 56 * 1024 * 1024

ROW_TILE = 512
ATTN_A_TQ = 128
ATTN_A_HEADS = 4
ATTN_C_T = 256


def _params(n_axes):
    return pltpu.CompilerParams(
        dimension_semantics=("arbitrary",) * n_axes,
        vmem_limit_bytes=V7X_VMEM_LIMIT_BYTES,
    )


def _resident(shape):
    nd = len(shape)
    return pl.BlockSpec(shape, lambda *_: (0,) * nd, pipeline_mode=pl.Buffered(1))


def _rms(x, g):
    return x * lax.rsqrt(jnp.mean(x * x, axis=-1, keepdims=True) + EPS) * g


def _mm(a, b):
    return jnp.dot(a, b, preferred_element_type=F32)


def _mm_nt(a, b):
    return lax.dot_general(a, b, (((1,), (1,)), ((), ())), preferred_element_type=F32)


def _ffn_kernel(x_ref, g_ref, win_ref, wout_ref, o_ref, *, g_row):
    x = x_ref[...]
    g = g_ref[...]
    xn = _rms(x, g[g_row:g_row + 1]).astype(BF16)
    gu = _mm(xn, win_ref[...])
    gate = gu[:, :D_FF]
    up = gu[:, D_FF:]
    h = (gate * (1.0 / (1.0 + jnp.exp(-gate))) * up).astype(BF16)
    y = _mm(h, wout_ref[...])
    o_ref[...] = x + 0.5 * _rms(y, g[g_row + 1:g_row + 2])


def _ffn(x, g, w_in, w_out, g_row):
    t, d = x.shape
    return pl.pallas_call(
        functools.partial(_ffn_kernel, g_row=g_row),
        grid=(t // ROW_TILE,),
        in_specs=[
            pl.BlockSpec((ROW_TILE, d), lambda i: (i, 0)),
            _resident(g.shape),
            _resident(w_in.shape),
            _resident(w_out.shape),
        ],
        out_specs=pl.BlockSpec((ROW_TILE, d), lambda i: (i, 0)),
        out_shape=jax.ShapeDtypeStruct((t, d), F32),
        compiler_params=_params(1),
        name="ffn",
    )(x, g, w_in, w_out)


def _norm_proj_kernel(x_ref, g_ref, w_ref, o_ref, *, g_row):
    xn = _rms(x_ref[...], g_ref[g_row:g_row + 1, :]).astype(BF16)
    o_ref[...] = _mm(xn, w_ref[...]).astype(o_ref.dtype)


def _norm_proj(x, g, w, g_row):
    t, d = x.shape
    n = w.shape[1]
    return pl.pallas_call(
        functools.partial(_norm_proj_kernel, g_row=g_row),
        grid=(t // ROW_TILE,),
        in_specs=[
            pl.BlockSpec((ROW_TILE, d), lambda i: (i, 0)),
            _resident(g.shape),
            _resident(w.shape),
        ],
        out_specs=pl.BlockSpec((ROW_TILE, n), lambda i: (i, 0)),
        out_shape=jax.ShapeDtypeStruct((t, n), BF16),
        compiler_params=_params(1),
        name="norm_proj",
    )(x, g, w)


def _proj_res_kernel(o_ref, w_ref, x_ref, g_ref, out_ref, *, g_row):
    m = _mm(o_ref[...], w_ref[...])
    out_ref[...] = x_ref[...] + _rms(m, g_ref[g_row:g_row + 1, :])


def _proj_res(o, w, x, g, g_row):
    t, d = x.shape
    return pl.pallas_call(
        functools.partial(_proj_res_kernel, g_row=g_row),
        grid=(t // ROW_TILE,),
        in_specs=[
            pl.BlockSpec((ROW_TILE, d), lambda i: (i, 0)),
            _resident(w.shape),
            pl.BlockSpec((ROW_TILE, d), lambda i: (i, 0)),
            _resident(g.shape),
        ],
        out_specs=pl.BlockSpec((ROW_TILE, d), lambda i: (i, 0)),
        out_shape=jax.ShapeDtypeStruct((t, d), F32),
        compiler_params=_params(1),
        name="proj_res",
    )(o, w, x, g)


def _attn_a_bias(rel_bias, tq, win):
    n_var = LEFT // tq + 1
    v = np.arange(n_var)[:, None, None]
    r = np.arange(tq)[None, :, None]
    j = np.arange(win)[None, None, :]
    qpos = v * tq + r
    qchunk = qpos // CHUNK
    valid = (j >= (qchunk - LEFT // CHUNK) * CHUNK) & (j < (qchunk + 1) * CHUNK)
    idx = np.clip(qpos - j, -REL_CLIP, REL_CLIP) + REL_CLIP
    bias = jnp.where(valid[None], rel_bias.astype(F32)[:, idx], NEG_INF)
    bias = jnp.transpose(bias, (1, 0, 2, 3))
    return bias.reshape(n_var, HEADS_A // 2, 2 * tq, win)


def _attn_a_kernel(q_ref, k_ref, v_ref, bias_ref, o_ref, *, tq, win):
    t = pl.program_id(2)
    ws = pl.multiple_of(jnp.maximum(t * tq - LEFT, 0), tq)
    lane = lax.broadcasted_iota(jnp.int32, (tq, HEAD_W), 1)
    first = lane < HEAD_W // 2
    for p in range(ATTN_A_HEADS // 2):
        cols = slice(p * HEAD_W, (p + 1) * HEAD_W)
        q = q_ref[0, :, cols] * (HEAD_W // 2) ** -0.5
        zero = jnp.zeros_like(q)
        qs = jnp.concatenate([jnp.where(first, q, zero), jnp.where(first, zero, q)], axis=0)
        kw = k_ref[0, pl.ds(ws, win), cols]
        s = _mm_nt(qs, kw) + bias_ref[0, p]
        m = jnp.max(s, axis=-1, keepdims=True)
        e = jnp.exp(s - m)
        l = jnp.sum(e, axis=-1, keepdims=True)
        vw = v_ref[0, pl.ds(ws, win), cols]
        o = _mm(e.astype(BF16), vw) / l
        o_ref[0, :, cols] = jnp.where(first, o[:tq], o[tq:]).astype(o_ref.dtype)


def _attn_a(qkv, bias):
    b, s, _ = qkv.shape
    tq = ATTN_A_TQ
    win = tq + LEFT
    gw = ATTN_A_HEADS * HEAD_W // 2
    n_groups = D_MODEL // gw
    n_special = LEFT // tq
    return pl.pallas_call(
        functools.partial(_attn_a_kernel, tq=tq, win=win),
        grid=(b, n_groups, s // tq),
        in_specs=[
            pl.BlockSpec((1, tq, gw), lambda bi, hg, t: (bi, t, hg)),
            pl.BlockSpec((1, s, gw), lambda bi, hg, t: (bi, 0, n_groups + hg)),
            pl.BlockSpec((1, s, gw), lambda bi, hg, t: (bi, 0, 2 * n_groups + hg)),
            pl.BlockSpec((1, ATTN_A_HEADS // 2, 2 * tq, win),
                         lambda bi, hg, t: (jnp.minimum(t, n_special), hg, 0, 0)),
        ],
        out_specs=pl.BlockSpec((1, tq, gw), lambda bi, hg, t: (bi, t, hg)),
        out_shape=jax.ShapeDtypeStruct((b, s, D_MODEL), BF16),
        compiler_params=_params(3),
        name="attn_a",
    )(qkv, qkv, qkv, bias)


def _gmlp_in_kernel(x_ref, g_ref, w_ref, lng_ref, lnb_ref, u_ref, v_ref, *, g_row):
    xn = _rms(x_ref[...], g_ref[g_row:g_row + 1, :]).astype(BF16)
    hw = _mm(xn, w_ref[...])
    act = 0.5 * hw * (1.0 + lax.erf(hw * math.sqrt(0.5)))
    u_ref[...] = act[:, :D_MODEL].astype(u_ref.dtype)
    v = act[:, D_MODEL:]
    mu = jnp.mean(v, axis=-1, keepdims=True)
    vc = v - mu
    var = jnp.mean(vc * vc, axis=-1, keepdims=True)
    v_ref[...] = (vc * lax.rsqrt(var + EPS) * lng_ref[...] + lnb_ref[...]).astype(v_ref.dtype)


def _gmlp_in(x, g, w, ln_g, ln_b, g_row):
    t, d = x.shape
    out = jax.ShapeDtypeStruct((t, d), BF16)
    row_spec = pl.BlockSpec((ROW_TILE, d), lambda i: (i, 0))
    return pl.pallas_call(
        functools.partial(_gmlp_in_kernel, g_row=g_row),
        grid=(t // ROW_TILE,),
        in_specs=[row_spec, _resident(g.shape), _resident(w.shape),
                  _resident(ln_g.shape), _resident(ln_b.shape)],
        out_specs=[row_spec, row_spec],
        out_shape=[out, out],
        compiler_params=_params(1),
        name="gmlp_in",
    )(x, g, w, ln_g, ln_b)


def _gmlp_out_kernel(u_ref, v_ref, ws_ref, bs_ref, wo_ref, x_ref, g_ref, out_ref, y_ref, *, g_row):
    c = GMLP_CHUNK
    row = lax.broadcasted_iota(jnp.int32, (c, c), 0)
    col = lax.broadcasted_iota(jnp.int32, (c, c), 1)
    causal = row >= col
    for grp in range(GMLP_GROUPS):
        w = ws_ref[grp]
        w = jnp.where(causal, w, jnp.zeros_like(w))
        b = bs_ref[:, grp:grp + 1]
        lanes = slice(grp * c, (grp + 1) * c)
        for blk in range(ROW_TILE // c):
            rows = slice(blk * c, (blk + 1) * c)
            sv = _mm(w, v_ref[rows, lanes]) + b
            y_ref[rows, lanes] = (u_ref[rows, lanes].astype(F32) * sv).astype(y_ref.dtype)
    m = _mm(y_ref[...], wo_ref[...])
    out_ref[...] = x_ref[...] + _rms(m, g_ref[g_row:g_row + 1, :])


def _gmlp_out(u, v, w_s, b_s_t, w_o, x, g, g_row):
    t, d = x.shape
    row_spec = pl.BlockSpec((ROW_TILE, d), lambda i: (i, 0))
    return pl.pallas_call(
        functools.partial(_gmlp_out_kernel, g_row=g_row),
        grid=(t // ROW_TILE,),
        in_specs=[row_spec, row_spec, _resident(w_s.shape), _resident(b_s_t.shape),
                  _resident(w_o.shape), row_spec, _resident(g.shape)],
        out_specs=row_spec,
        out_shape=jax.ShapeDtypeStruct((t, d), F32),
        scratch_shapes=[pltpu.VMEM((ROW_TILE, d), BF16)],
        compiler_params=_params(1),
        name="gmlp_out",
    )(u, v, w_s, b_s_t, w_o, x, g)


def _attn_c_kernel(lam_ref, subg_ref, q_ref, k_ref, v_ref, o_ref, m_ref, l_ref, acc_ref,
                   *, tile, lambda_init):
    h = pl.program_id(1)
    t = pl.program_id(2)
    lane = lax.broadcasted_iota(jnp.int32, (tile, HEAD_W), 1)
    first = lane < HEAD_W // 2
    q = q_ref[0] * (HEAD_W // 2) ** -0.5
    zero = jnp.zeros_like(q)
    qs = jnp.concatenate([jnp.where(first, q, zero), jnp.where(first, zero, q)], axis=0)
    inv_slope = jnp.left_shift(1, jnp.full((1, tile), h + 1, jnp.int32)).astype(F32)
    kcol = lax.broadcasted_iota(jnp.int32, (1, tile), 1)

    m_ref[...] = jnp.full(m_ref.shape, NEG_INF, F32)
    l_ref[...] = jnp.zeros(l_ref.shape, F32)
    acc_ref[...] = jnp.zeros(acc_ref.shape, F32)

    def update(s, vb):
        m_old = m_ref[...]
        m_new = jnp.maximum(m_old, jnp.max(s, axis=-1, keepdims=True))
        alpha = jnp.exp(m_old - m_new)
        e = jnp.exp(s - m_new)
        l_ref[...] = alpha * l_ref[...] + jnp.sum(e, axis=-1, keepdims=True)
        acc_ref[...] = alpha * acc_ref[...] + _mm(e.astype(BF16), vb)
        m_ref[...] = m_new

    def past_block(j, carry):
        start = pl.multiple_of(j * tile, tile)
        s = _mm_nt(qs, k_ref[0, pl.ds(start, tile), :])
        kb = ((j - t) * tile + kcol).astype(F32) / inv_slope
        update(s + kb, v_ref[0, pl.ds(start, tile), :])
        return carry

    lax.fori_loop(0, t, past_block, 0)

    start = pl.multiple_of(t * tile, tile)
    s = _mm_nt(qs, k_ref[0, pl.ds(start, tile), :])
    qrow = lax.broadcasted_iota(jnp.int32, (2 * tile, tile), 0) & (tile - 1)
    kc = lax.broadcasted_iota(jnp.int32, (2 * tile, tile), 1)
    alibi = (qrow - jnp.abs(qrow - kc)).astype(F32) / inv_slope
    allowed = (kc // CHUNK) <= (qrow // CHUNK)
    update(jnp.where(allowed, s + alibi, NEG_INF), v_ref[0, pl.ds(start, tile), :])

    o = acc_ref[...] / l_ref[...]
    lam = lam_ref[...]
    lam_full = (jnp.exp(jnp.sum(lam[0:1] * lam[1:2], axis=-1, keepdims=True))
                - jnp.exp(jnp.sum(lam[2:3] * lam[3:4], axis=-1, keepdims=True)) + lambda_init)
    a = o[:tile] - lam_full * o[tile:]
    o_ref[0] = (_rms(a, subg_ref[...]) * (1.0 - lambda_init)).astype(o_ref.dtype)


def _attn_c(qkv, lam, subln_g, lambda_init):
    b, s, _ = qkv.shape
    tile = ATTN_C_T
    return pl.pallas_call(
        functools.partial(_attn_c_kernel, tile=tile, lambda_init=lambda_init),
        grid=(b, HEADS_C, s // tile),
        in_specs=[
            _resident(lam.shape),
            _resident(subln_g.shape),
            pl.BlockSpec((1, tile, HEAD_W), lambda bi, h, t: (bi, t, h)),
            pl.BlockSpec((1, s, HEAD_W), lambda bi, h, t: (bi, 0, HEADS_C + h)),
            pl.BlockSpec((1, s, HEAD_W), lambda bi, h, t: (bi, 0, 2 * HEADS_C + h)),
        ],
        out_specs=pl.BlockSpec((1, tile, HEAD_W), lambda bi, h, t: (bi, t, h)),
        out_shape=jax.ShapeDtypeStruct((b, s, D_MODEL), BF16),
        scratch_shapes=[
            pltpu.VMEM((2 * tile, 1), F32),
            pltpu.VMEM((2 * tile, 1), F32),
            pltpu.VMEM((2 * tile, HEAD_W), F32),
        ],
        compiler_params=_params(3),
        name="attn_c",
    )(lam, subln_g, qkv, qkv, qkv)


def kernel(x, norm_g, ff1_w_in, ff1_w_out, ff2_w_in, ff2_w_out, a_w_qkv, a_rel_bias, a_w_o,
           b_w_in, b_ln_g, b_ln_b, b_w_s, b_b_s, b_w_o, c_w_qkv, c_lambda, c_subln_g, c_w_o):
    b, s, d = x.shape
    depth = norm_g.shape[0]
    xf = x.reshape(b * s, d)
    for i in range(depth):
        g = norm_g[i]
        xf = _ffn(xf, g, ff1_w_in[i].astype(BF16), ff1_w_out[i].astype(BF16), 0)
        kind, j = i % N_MIXERS, i // N_MIXERS
        if kind == 0:
            qkv = _norm_proj(xf, g, a_w_qkv[j].astype(BF16), 2)
            bias = _attn_a_bias(a_rel_bias[j], ATTN_A_TQ, ATTN_A_TQ + LEFT)
            o = _attn_a(qkv.reshape(b, s, 3 * d), bias)
            xf = _proj_res(o.reshape(b * s, d), a_w_o[j].astype(BF16), xf, g, 3)
        elif kind == 1:
            u, v = _gmlp_in(xf, g, b_w_in[j].astype(BF16), b_ln_g[j][None], b_ln_b[j][None], 2)
            xf = _gmlp_out(u, v, b_w_s[j].astype(BF16), b_b_s[j].T, b_w_o[j].astype(BF16), xf, g, 3)
        else:
            lambda_init = 0.8 - 0.6 * math.exp(-0.3 * i)
            qkv = _norm_proj(xf, g, c_w_qkv[j].astype(BF16), 2)
            o = _attn_c(qkv.reshape(b, s, 3 * d), c_lambda[j], c_subln_g[j][None], lambda_init)
            xf = _proj_res(o.reshape(b * s, d), c_w_o[j].astype(BF16), xf, g, 3)
        xf = _ffn(xf, g, ff2_w_in[i].astype(BF16), ff2_w_out[i].astype(BF16), 4)
    return xf.reshape(b, s, d)
```

```python
import functools
import math

import jax
import jax.numpy as jnp
import numpy as np
from jax import lax
from jax.experimental import pallas as pl
from jax.experimental.pallas import tpu as pltpu

F32 = jnp.float32
BF16 = jnp.bfloat16

D_MODEL = 1024
D_FF = 2816
EPS = 1e-6
NEG_INF = -1e30
N_MIXERS = 3

CHUNK = 64
LEFT = 8 * CHUNK
HEADS_A = 16
REL_CLIP = 128
GMLP_CHUNK = 128
GMLP_GROUPS = 8
HEADS_C = 8
HEAD_W = 128

V7X_VMEM_LIMIT_BYTES = 56 * 1024 * 1024

ROW_TILE = 512
ATTN_A_TQ = 128
ATTN_A_HEADS = 4
ATTN_C_TILE = 512


def _params(n_axes):
    return pltpu.CompilerParams(
        dimension_semantics=("arbitrary",) * n_axes,
        vmem_limit_bytes=V7X_VMEM_LIMIT_BYTES,
    )


def _resident(shape):
    nd = len(shape)
    return pl.BlockSpec(shape, lambda *_: (0,) * nd, pipeline_mode=pl.Buffered(1))


def _rms(x, g):
    return x * lax.rsqrt(jnp.mean(x * x, axis=-1, keepdims=True) + EPS) * g


def _mm(a, b):
    return jnp.dot(a, b, preferred_element_type=F32)


def _mm_nt(a, b):
    return lax.dot_general(a, b, (((1,), (1,)), ((), ())), preferred_element_type=F32)


def _ffn_kernel(x_ref, g_ref, win_ref, wout_ref, o_ref, *, g_row):
    x = x_ref[...]
    g = g_ref[...]
    xn = _rms(x, g[g_row:g_row + 1]).astype(BF16)
    gu = _mm(xn, win_ref[...])
    gate = gu[:, :D_FF]
    up = gu[:, D_FF:]
    h = (gate * (1.0 / (1.0 + jnp.exp(-gate))) * up).astype(BF16)
    y = _mm(h, wout_ref[...])
    o_ref[...] = x + 0.5 * _rms(y, g[g_row + 1:g_row + 2])


def _ffn(x, g, w_in, w_out, g_row):
    t, d = x.shape
    return pl.pallas_call(
        functools.partial(_ffn_kernel, g_row=g_row),
        grid=(t // ROW_TILE,),
        in_specs=[
            pl.BlockSpec((ROW_TILE, d), lambda i: (i, 0)),
            _resident(g.shape),
            _resident(w_in.shape),
            _resident(w_out.shape),
        ],
        out_specs=pl.BlockSpec((ROW_TILE, d), lambda i: (i, 0)),
        out_shape=jax.ShapeDtypeStruct((t, d), F32),
        compiler_params=_params(1),
        name="ffn",
    )(x, g, w_in, w_out)


def _norm_proj_kernel(x_ref, g_ref, w_ref, o_ref, *, g_row):
    xn = _rms(x_ref[...], g_ref[g_row:g_row + 1, :]).astype(BF16)
    o_ref[...] = _mm(xn, w_ref[...]).astype(o_ref.dtype)


def _norm_proj(x, g, w, g_row):
    t, d = x.shape
    n = w.shape[1]
    return pl.pallas_call(
        functools.partial(_norm_proj_kernel, g_row=g_row),
        grid=(t // ROW_TILE,),
        in_specs=[
            pl.BlockSpec((ROW_TILE, d), lambda i: (i, 0)),
            _resident(g.shape),
            _resident(w.shape),
        ],
        out_specs=pl.BlockSpec((ROW_TILE, n), lambda i: (i, 0)),
        out_shape=jax.ShapeDtypeStruct((t, n), BF16),
        compiler_params=_params(1),
        name="norm_proj",
    )(x, g, w)


def _proj_res_kernel(o_ref, w_ref, x_ref, g_ref, out_ref, *, g_row):
    m = _mm(o_ref[...], w_ref[...])
    out_ref[...] = x_ref[...] + _rms(m, g_ref[g_row:g_row + 1, :])


def _proj_res(o, w, x, g, g_row):
    t, d = x.shape
    return pl.pallas_call(
        functools.partial(_proj_res_kernel, g_row=g_row),
        grid=(t // ROW_TILE,),
        in_specs=[
            pl.BlockSpec((ROW_TILE, d), lambda i: (i, 0)),
            _resident(w.shape),
            pl.BlockSpec((ROW_TILE, d), lambda i: (i, 0)),
            _resident(g.shape),
        ],
        out_specs=pl.BlockSpec((ROW_TILE, d), lambda i: (i, 0)),
        out_shape=jax.ShapeDtypeStruct((t, d), F32),
        compiler_params=_params(1),
        name="proj_res",
    )(o, w, x, g)


def _attn_a_bias(rel_bias, tq, win):
    n_var = LEFT // tq + 1
    heads = rel_bias.shape[0]
    tbl = rel_bias.astype(F32)
    n_hi = tq + LEFT - REL_CLIP
    n_lo = win - REL_CLIP
    f = jnp.concatenate([
        jnp.broadcast_to(tbl[:, 2 * REL_CLIP:], (heads, n_hi)),
        tbl[:, 2 * REL_CLIP - 1:0:-1],
        jnp.broadcast_to(tbl[:, :1], (heads, n_lo)),
    ], axis=1)
    p = win + tq
    tiles = []
    for v in range(n_var):
        shift = LEFT - v * tq
        w = jnp.pad(f[:, shift:shift + p - 1], ((0, 0), (0, 1)))
        skew = jnp.tile(w, (1, tq))[:, :tq * (p - 1)].reshape(heads, tq, p - 1)
        tiles.append(skew[:, :, tq - 1:tq - 1 + win])
    bias = jnp.stack(tiles, axis=0)
    v = np.arange(n_var)[:, None, None]
    r = np.arange(tq)[None, :, None]
    j = np.arange(win)[None, None, :]
    qchunk = (v * tq + r) // CHUNK
    valid = (j >= (qchunk - LEFT // CHUNK) * CHUNK) & (j < (qchunk + 1) * CHUNK)
    bias = jnp.where(valid[:, None], bias, NEG_INF)
    return bias.reshape(n_var, HEADS_A // 2, 2 * tq, win)


def _attn_a_kernel(q_ref, k_ref, v_ref, bias_ref, o_ref, *, tq, win):
    t = pl.program_id(2)
    ws = pl.multiple_of(jnp.maximum(t * tq - LEFT, 0), tq)
    lane = lax.broadcasted_iota(jnp.int32, (tq, HEAD_W), 1)
    first = lane < HEAD_W // 2
    for p in range(ATTN_A_HEADS // 2):
        cols = slice(p * HEAD_W, (p + 1) * HEAD_W)
        q = q_ref[0, :, cols] * (HEAD_W // 2) ** -0.5
        zero = jnp.zeros_like(q)
        qs = jnp.concatenate([jnp.where(first, q, zero), jnp.where(first, zero, q)], axis=0)
        kw = k_ref[0, pl.ds(ws, win), cols]
        s = _mm_nt(qs, kw) + bias_ref[0, p]
        m = jnp.max(s, axis=-1, keepdims=True)
        e = jnp.exp(s - m)
        l = jnp.sum(e, axis=-1, keepdims=True)
        vw = v_ref[0, pl.ds(ws, win), cols]
        o = _mm(e.astype(BF16), vw) / l
        o_ref[0, :, cols] = jnp.where(first, o[:tq], o[tq:]).astype(o_ref.dtype)


def _attn_a(qkv, bias):
    b, s, _ = qkv.shape
    tq = ATTN_A_TQ
    win = tq + LEFT
    gw = ATTN_A_HEADS * HEAD_W // 2
    n_groups = D_MODEL // gw
    n_special = LEFT // tq
    return pl.pallas_call(
        functools.partial(_attn_a_kernel, tq=tq, win=win),
        grid=(b, n_groups, s // tq),
        in_specs=[
            pl.BlockSpec((1, tq, gw), lambda bi, hg, t: (bi, t, hg)),
            pl.BlockSpec((1, s, gw), lambda bi, hg, t: (bi, 0, n_groups + hg)),
            pl.BlockSpec((1, s, gw), lambda bi, hg, t: (bi, 0, 2 * n_groups + hg)),
            pl.BlockSpec((1, ATTN_A_HEADS // 2, 2 * tq, win),
                         lambda bi, hg, t: (jnp.minimum(t, n_special), hg, 0, 0)),
        ],
        out_specs=pl.BlockSpec((1, tq, gw), lambda bi, hg, t: (bi, t, hg)),
        out_shape=jax.ShapeDtypeStruct((b, s, D_MODEL), BF16),
        compiler_params=_params(3),
        name="attn_a",
    )(qkv, qkv, qkv, bias)


def _gmlp_in_kernel(x_ref, g_ref, w_ref, lng_ref, lnb_ref, u_ref, v_ref, *, g_row):
    xn = _rms(x_ref[...], g_ref[g_row:g_row + 1, :]).astype(BF16)
    hw = _mm(xn, w_ref[...])
    act = 0.5 * hw * (1.0 + lax.erf(hw * math.sqrt(0.5)))
    u_ref[...] = act[:, :D_MODEL].astype(u_ref.dtype)
    v = act[:, D_MODEL:]
    mu = jnp.mean(v, axis=-1, keepdims=True)
    vc = v - mu
    var = jnp.mean(vc * vc, axis=-1, keepdims=True)
    v_ref[...] = (vc * lax.rsqrt(var + EPS) * lng_ref[...] + lnb_ref[...]).astype(v_ref.dtype)


def _gmlp_in(x, g, w, ln_g, ln_b, g_row):
    t, d = x.shape
    out = jax.ShapeDtypeStruct((t, d), BF16)
    row_spec = pl.BlockSpec((ROW_TILE, d), lambda i: (i, 0))
    return pl.pallas_call(
        functools.partial(_gmlp_in_kernel, g_row=g_row),
        grid=(t // ROW_TILE,),
        in_specs=[row_spec, _resident(g.shape), _resident(w.shape),
                  _resident(ln_g.shape), _resident(ln_b.shape)],
        out_specs=[row_spec, row_spec],
        out_shape=[out, out],
        compiler_params=_params(1),
        name="gmlp_in",
    )(x, g, w, ln_g, ln_b)


def _gmlp_out_kernel(u_ref, v_ref, ws_ref, bs_ref, wo_ref, x_ref, g_ref, out_ref, y_ref, *, g_row):
    c = GMLP_CHUNK
    row = lax.broadcasted_iota(jnp.int32, (c, c), 0)
    col = lax.broadcasted_iota(jnp.int32, (c, c), 1)
    causal = row >= col
    for grp in range(GMLP_GROUPS):
        w = ws_ref[grp]
        w = jnp.where(causal, w, jnp.zeros_like(w))
        b = bs_ref[:, grp:grp + 1]
        lanes = slice(grp * c, (grp + 1) * c)
        for blk in range(ROW_TILE // c):
            rows = slice(blk * c, (blk + 1) * c)
            sv = _mm(w, v_ref[rows, lanes]) + b
            y_ref[rows, lanes] = (u_ref[rows, lanes].astype(F32) * sv).astype(y_ref.dtype)
    m = _mm(y_ref[...], wo_ref[...])
    out_ref[...] = x_ref[...] + _rms(m, g_ref[g_row:g_row + 1, :])


def _gmlp_out(u, v, w_s, b_s_t, w_o, x, g, g_row):
    t, d = x.shape
    row_spec = pl.BlockSpec((ROW_TILE, d), lambda i: (i, 0))
    return pl.pallas_call(
        functools.partial(_gmlp_out_kernel, g_row=g_row),
        grid=(t // ROW_TILE,),
        in_specs=[row_spec, row_spec, _resident(w_s.shape), _resident(b_s_t.shape),
                  _resident(w_o.shape), row_spec, _resident(g.shape)],
        out_specs=row_spec,
        out_shape=jax.ShapeDtypeStruct((t, d), F32),
        scratch_shapes=[pltpu.VMEM((ROW_TILE, d), BF16)],
        compiler_params=_params(1),
        name="gmlp_out",
    )(u, v, w_s, b_s_t, w_o, x, g)


def _norm_proj_c_kernel(x_ref, g_ref, wk_ref, wqv_ref, qt_ref, k_ref, vt_ref, *, g_row, tq):
    xn = _rms(x_ref[...], g_ref[g_row:g_row + 1, :]).astype(BF16)
    k_ref[...] = _mm(xn, wk_ref[...]).astype(k_ref.dtype)
    qv_t = _mm_nt(wqv_ref[...], xn).astype(BF16)
    rows = xn.shape[0]
    for part in range(rows // tq):
        cols = slice(part * tq, (part + 1) * tq)
        qt_ref[0, :, part] = qv_t[:D_MODEL, cols].reshape(HEADS_C, HEAD_W, tq)
    vt_ref[0, :, 0] = qv_t[D_MODEL:].reshape(HEADS_C, HEAD_W, rows)


def _norm_proj_c(x, g, w_k, w_qv_t, g_row, batch):
    t, d = x.shape
    s = t // batch
    tq = tk = ATTN_C_TILE
    assert ROW_TILE == tk
    per_batch = s // ROW_TILE
    return pl.pallas_call(
        functools.partial(_norm_proj_c_kernel, g_row=g_row, tq=tq),
        grid=(t // ROW_TILE,),
        in_specs=[
            pl.BlockSpec((ROW_TILE, d), lambda i: (i, 0)),
            _resident(g.shape),
            _resident(w_k.shape),
            _resident(w_qv_t.shape),
        ],
        out_specs=[
            pl.BlockSpec((1, HEADS_C, tk // tq, HEAD_W, tq),
                         lambda i: (i // per_batch, 0, i % per_batch, 0, 0)),
            pl.BlockSpec((ROW_TILE, d), lambda i: (i, 0)),
            pl.BlockSpec((1, HEADS_C, 1, HEAD_W, tk),
                         lambda i: (i // per_batch, 0, i % per_batch, 0, 0)),
        ],
        out_shape=[
            jax.ShapeDtypeStruct((batch, HEADS_C, s // tq, HEAD_W, tq), BF16),
            jax.ShapeDtypeStruct((t, d), BF16),
            jax.ShapeDtypeStruct((batch, HEADS_C, s // tk, HEAD_W, tk), BF16),
        ],
        compiler_params=_params(1),
        name="norm_proj_c",
    )(x, g, w_k, w_qv_t)


def _attn_c_kernel(lam_ref, subg_ref, qt_ref, k_ref, vt_ref, o_ref,
                   qs_ref, s_ref, p_ref, kb_ref, diag_ref, m_ref, l_ref, acc_ref,
                   *, tile, lambda_init):
    h = pl.program_id(1)
    t = pl.program_id(2)
    n = 2 * tile
    lanes = HEAD_W
    group = tile
    inv_slope = jnp.left_shift(1, jnp.full((1, lanes), h + 1, jnp.int32)).astype(F32)

    @pl.when(t == 0)
    def _():
        key = lax.broadcasted_iota(jnp.int32, (tile, lanes), 0)
        kb_ref[...] = key.astype(F32) / inv_slope
        for c in range(tile // lanes):
            qry = lax.broadcasted_iota(jnp.int32, (tile, lanes), 1) + c * lanes
            alibi = (qry - jnp.abs(qry - key)).astype(F32) / inv_slope
            allowed = jnp.right_shift(key, 6) <= jnp.right_shift(qry, 6)
            diag_ref[:, c * lanes:(c + 1) * lanes] = jnp.where(allowed, alibi, NEG_INF)

    qt = qt_ref[0, 0, 0] * (HEAD_W // 2) ** -0.5
    first = lax.broadcasted_iota(jnp.int32, (HEAD_W, tile), 0) < HEAD_W // 2
    zero = jnp.zeros_like(qt)
    qs_ref[:, :tile] = jnp.where(first, qt, zero)
    qs_ref[:, tile:] = jnp.where(first, zero, qt)

    m_ref[...] = jnp.full(m_ref.shape, NEG_INF, F32)
    l_ref[...] = jnp.zeros(l_ref.shape, F32)
    acc_ref[...] = jnp.zeros(acc_ref.shape, F32)

    def issue_scores(j, g):
        start = pl.multiple_of(j * tile, tile)
        gcols = slice(g * group, (g + 1) * group)
        s_ref[:, gcols] = _mm(k_ref[0, pl.ds(start, tile), :], qs_ref[:, gcols])

    def softmax_pv(j, g, bias, shift):
        gcols = slice(g * group, (g + 1) * group)
        alphas = []
        for c in range(g * group // lanes, (g + 1) * group // lanes):
            cols = slice(c * lanes, (c + 1) * lanes)
            sc = s_ref[:, cols] + bias(c)
            s_ref[:, cols] = sc
            m_old = m_ref[:, cols]
            m_new = jnp.maximum(m_old, jnp.max(sc, axis=0, keepdims=True) + shift)
            alpha = jnp.exp(m_old - m_new)
            e = jnp.exp(s_ref[:, cols] - (m_new - shift))
            l_ref[:, cols] = alpha * l_ref[:, cols] + jnp.sum(e, axis=0, keepdims=True)
            m_ref[:, cols] = m_new
            p_ref[:, cols] = e.astype(BF16)
            alphas.append(alpha)
        alpha_g = jnp.concatenate(alphas, axis=1)
        acc_ref[:, gcols] = alpha_g * acc_ref[:, gcols] + _mm(vt_ref[0, 0, j], p_ref[:, gcols])

    issue_scores(0, 0)

    def past_block(j, carry):
        shift = jnp.full((1, lanes), (j - t) * tile, jnp.int32).astype(F32) / inv_slope
        issue_scores(j, 1)
        softmax_pv(j, 0, lambda c: kb_ref[...], shift)
        issue_scores(j + 1, 0)
        softmax_pv(j, 1, lambda c: kb_ref[...], shift)
        return carry

    lax.fori_loop(0, t, past_block, 0)

    def diag_bias(c):
        q0 = (c * lanes) % tile
        return diag_ref[:, q0:q0 + lanes]

    no_shift = jnp.zeros((1, lanes), F32)
    issue_scores(t, 1)
    softmax_pv(t, 0, diag_bias, no_shift)
    softmax_pv(t, 1, diag_bias, no_shift)

    o_t = acc_ref[...] / l_ref[...]
    lam = lam_ref[...]
    lam_full = (jnp.exp(jnp.sum(lam[0:1] * lam[1:2], axis=-1, keepdims=True))
                - jnp.exp(jnp.sum(lam[2:3] * lam[3:4], axis=-1, keepdims=True)) + lambda_init)
    a = (o_t[:, :tile] - lam_full * o_t[:, tile:]).T
    o_ref[0] = (_rms(a, subg_ref[...]) * (1.0 - lambda_init)).astype(o_ref.dtype)


def _attn_c(q_t, k, v_t, lam, subln_g, lambda_init):
    b, s, _ = k.shape
    tile = ATTN_C_TILE
    n = 2 * tile
    return pl.pallas_call(
        functools.partial(_attn_c_kernel, tile=tile, lambda_init=lambda_init),
        grid=(b, HEADS_C, s // tile),
        in_specs=[
            _resident(lam.shape),
            _resident(subln_g.shape),
            pl.BlockSpec((1, 1, 1, HEAD_W, tile), lambda bi, h, t: (bi, h, t, 0, 0)),
            pl.BlockSpec((1, s, HEAD_W), lambda bi, h, t: (bi, 0, h)),
            pl.BlockSpec((1, 1, s // tile, HEAD_W, tile), lambda bi, h, t: (bi, h, 0, 0, 0)),
        ],
        out_specs=pl.BlockSpec((1, tile, HEAD_W), lambda bi, h, t: (bi, t, h)),
        out_shape=jax.ShapeDtypeStruct((b, s, D_MODEL), BF16),
        scratch_shapes=[
            pltpu.VMEM((HEAD_W, n), BF16),
            pltpu.VMEM((tile, n), F32),
            pltpu.VMEM((tile, n), BF16),
            pltpu.VMEM((tile, HEAD_W), F32),
            pltpu.VMEM((tile, tile), F32),
            pltpu.VMEM((1, n), F32),
            pltpu.VMEM((1, n), F32),
            pltpu.VMEM((HEAD_W, n), F32),
        ],
        compiler_params=_params(3),
        name="attn_c",
    )(lam, subln_g, q_t, k, v_t)


def kernel(x, norm_g, ff1_w_in, ff1_w_out, ff2_w_in, ff2_w_out, a_w_qkv, a_rel_bias, a_w_o,
           b_w_in, b_ln_g, b_ln_b, b_w_s, b_b_s, b_w_o, c_w_qkv, c_lambda, c_subln_g, c_w_o):
    b, s, d = x.shape
    depth = norm_g.shape[0]
    xf = x.reshape(b * s, d)
    for i in range(depth):
        g = norm_g[i]
        xf = _ffn(xf, g, ff1_w_in[i].astype(BF16), ff1_w_out[i].astype(BF16), 0)
        kind, j = i % N_MIXERS, i // N_MIXERS
        if kind == 0:
            qkv = _norm_proj(xf, g, a_w_qkv[j].astype(BF16), 2)
            bias = _attn_a_bias(a_rel_bias[j], ATTN_A_TQ, ATTN_A_TQ + LEFT)
            o = _attn_a(qkv.reshape(b, s, 3 * d), bias)
            xf = _proj_res(o.reshape(b * s, d), a_w_o[j].astype(BF16), xf, g, 3)
        elif kind == 1:
            u, v = _gmlp_in(xf, g, b_w_in[j].astype(BF16), b_ln_g[j][None], b_ln_b[j][None], 2)
            xf = _gmlp_out(u, v, b_w_s[j].astype(BF16), b_b_s[j].T, b_w_o[j].astype(BF16), xf, g, 3)
        else:
            lambda_init = 0.8 - 0.6 * math.exp(-0.3 * i)
            w = c_w_qkv[j].astype(BF16)
            w_qv_t = jnp.concatenate([w[:, :d], w[:, 2 * d:]], axis=1).T
            q_t, k, v_t = _norm_proj_c(xf, g, w[:, d:2 * d], w_qv_t, 2, b)
            o = _attn_c(q_t, k.reshape(b, s, d), v_t, c_lambda[j], c_subln_g[j][None], lambda_init)
            xf = _proj_res(o.reshape(b * s, d), c_w_o[j].astype(BF16), xf, g, 3)
        xf = _ffn(xf, g, ff2_w_in[i].astype(BF16), ff2_w_out[i].astype(BF16), 4)
    return xf.reshape(b, s, d)
```

```python
import functools
import math
from typing import NamedTuple

import jax
import jax.numpy as jnp
import numpy as np
from jax import lax
from jax.experimental import pallas as pl
from jax.experimental.pallas import tpu as pltpu

F32 = jnp.float32
BF16 = jnp.bfloat16

D_MODEL = 1024
D_FF = 2816
EPS = 1e-6
NEG_INF = -1e30
N_MIXERS = 3

CHUNK = 64
LEFT = 8 * CHUNK
HEADS_A = 16
REL_CLIP = 128
GMLP_CHUNK = 128
GMLP_GROUPS = 8
HEADS_C = 8
HEAD_W = 128
LOG2_E = math.log2(math.e)
LOG2_E_BF16 = (1.4453125, -0.00262451171875, 7.063150405883789e-06)

V7X_VMEM_LIMIT_BYTES = 56 * 1024 * 1024

ROW_TILE = 512
ATTN_A_TQ = 128
ATTN_A_HEADS = 8
ATTN_C_TILE = 512


def _params(n_axes):
    return pltpu.CompilerParams(
        dimension_semantics=("arbitrary",) * n_axes,
        vmem_limit_bytes=V7X_VMEM_LIMIT_BYTES,
    )


class _Layer(NamedTuple):
    stack: jax.Array
    index: int


def _resident(p):
    if isinstance(p, _Layer):
        tail = p.stack.shape[1:]
        return pl.BlockSpec((None,) + tail, lambda *_: (p.index,) + (0,) * len(tail),
                            pipeline_mode=pl.Buffered(1))
    return pl.BlockSpec(p.shape, lambda *_: (0,) * p.ndim, pipeline_mode=pl.Buffered(1))


def _operands(*params):
    return [p.stack if isinstance(p, _Layer) else p for p in params]


def _rms(x, g):
    return x * lax.rsqrt(jnp.mean(x * x, axis=-1, keepdims=True) + EPS) * g


def _mm(a, b):
    return jnp.dot(a, b, preferred_element_type=F32)


def _mm_nt(a, b):
    return lax.dot_general(a, b, (((1,), (1,)), ((), ())), preferred_element_type=F32)


def _ffn_kernel(x_ref, g_ref, win_ref, wout_ref, o_ref, *, g_row):
    x = x_ref[...]
    g = g_ref[...]
    xn = _rms(x, g[g_row:g_row + 1]).astype(BF16)
    gu = _mm(xn, win_ref[...])
    gate = gu[:, :D_FF]
    up = gu[:, D_FF:]
    h = (gate * (1.0 / (1.0 + jnp.exp(-gate))) * up).astype(BF16)
    y = _mm(h, wout_ref[...])
    o_ref[...] = x + 0.5 * _rms(y, g[g_row + 1:g_row + 2])


def _ffn(x, g, w_in, w_out, g_row):
    t, d = x.shape
    return pl.pallas_call(
        functools.partial(_ffn_kernel, g_row=g_row),
        grid=(t // ROW_TILE,),
        in_specs=[
            pl.BlockSpec((ROW_TILE, d), lambda i: (i, 0)),
            _resident(g),
            _resident(w_in),
            _resident(w_out),
        ],
        out_specs=pl.BlockSpec((ROW_TILE, d), lambda i: (i, 0)),
        out_shape=jax.ShapeDtypeStruct((t, d), F32),
        compiler_params=_params(1),
        name="ffn",
    )(x, *_operands(g, w_in, w_out))


def _norm_proj_kernel(x_ref, g_ref, w_ref, o_ref, *, g_row):
    xn = _rms(x_ref[...], g_ref[g_row:g_row + 1, :]).astype(BF16)
    qkv = _mm(xn, w_ref[...])
    q_scale = (HEAD_W // 2) ** -0.5 * LOG2_E
    o_ref[:, :D_MODEL] = (qkv[:, :D_MODEL] * q_scale).astype(o_ref.dtype)
    o_ref[:, D_MODEL:] = qkv[:, D_MODEL:].astype(o_ref.dtype)


def _norm_proj(x, g, w, g_row):
    t, d = x.shape
    n = w.stack.shape[-1]
    return pl.pallas_call(
        functools.partial(_norm_proj_kernel, g_row=g_row),
        grid=(t // ROW_TILE,),
        in_specs=[
            pl.BlockSpec((ROW_TILE, d), lambda i: (i, 0)),
            _resident(g),
            _resident(w),
        ],
        out_specs=pl.BlockSpec((ROW_TILE, n), lambda i: (i, 0)),
        out_shape=jax.ShapeDtypeStruct((t, n), BF16),
        compiler_params=_params(1),
        name="norm_proj",
    )(x, *_operands(g, w))


def _proj_res_kernel(o_ref, w_ref, x_ref, g_ref, out_ref, *, g_row):
    m = _mm(o_ref[...], w_ref[...])
    out_ref[...] = x_ref[...] + _rms(m, g_ref[g_row:g_row + 1, :])


def _proj_res(o, w, x, g, g_row):
    t, d = x.shape
    return pl.pallas_call(
        functools.partial(_proj_res_kernel, g_row=g_row),
        grid=(t // ROW_TILE,),
        in_specs=[
            pl.BlockSpec((ROW_TILE, d), lambda i: (i, 0)),
            _resident(w),
            pl.BlockSpec((ROW_TILE, d), lambda i: (i, 0)),
            _resident(g),
        ],
        out_specs=pl.BlockSpec((ROW_TILE, d), lambda i: (i, 0)),
        out_shape=jax.ShapeDtypeStruct((t, d), F32),
        compiler_params=_params(1),
        name="proj_res",
    )(o, *_operands(w), x, *_operands(g))


def _attn_a_bias(rel_bias, tq, win):
    heads = rel_bias.shape[0]
    tbl = rel_bias.astype(F32)
    f = jnp.concatenate([
        jnp.broadcast_to(tbl[:, 2 * REL_CLIP:], (heads, tq + LEFT - REL_CLIP)),
        tbl[:, 2 * REL_CLIP - 1:0:-1],
    ], axis=1)
    p = win + tq
    skew = jnp.tile(jnp.pad(f, ((0, 0), (0, 1))), (1, tq))[:, :tq * (p - 1)]
    bias = skew.reshape(heads, tq, p - 1)[:, :, tq - 1:tq - 1 + win]
    r = np.arange(tq)[:, None] // CHUNK
    j = np.arange(win)[None, :]
    valid = (j >= r * CHUNK) & (j < (r + LEFT // CHUNK + 1) * CHUNK)
    bias = jnp.where(valid, bias * LOG2_E, NEG_INF)
    bias = jnp.pad(bias, ((0, 0), (0, 0), (0, LEFT)), constant_values=NEG_INF)
    bias = bias.reshape(HEADS_A // 2, 2 * tq, (win + LEFT) // HEAD_W, HEAD_W)
    return jnp.transpose(bias, (0, 2, 1, 3))


def _attn_a_kernel(q_ref, k_ref, v_ref, bias_ref, o_ref, *, tq, win):
    t = pl.program_id(2)
    ws = pl.multiple_of(jnp.maximum(t * tq - LEFT, 0), tq)
    shift = jnp.maximum(LEFT // HEAD_W - t * (tq // HEAD_W), 0)
    lane = lax.broadcasted_iota(jnp.int32, (tq, HEAD_W), 1)
    first = lane < HEAD_W // 2
    pairs = ATTN_A_HEADS // 2

    def scores(p):
        cols = slice(p * HEAD_W, (p + 1) * HEAD_W)
        q = q_ref[0, :, cols]
        zero = jnp.zeros_like(q)
        qs = jnp.concatenate([jnp.where(first, q, zero), jnp.where(first, zero, q)], axis=0)
        bias = jnp.concatenate([bias_ref[p, shift + c] for c in range(win // HEAD_W)], axis=1)
        return _mm_nt(qs, k_ref[0, pl.ds(ws, win), cols]) + bias

    def attend(p, s):
        cols = slice(p * HEAD_W, (p + 1) * HEAD_W)
        m = jnp.max(s, axis=-1, keepdims=True)
        e = jnp.exp2(s - m)
        l = jnp.sum(e, axis=-1, keepdims=True)
        o = _mm(e.astype(BF16), v_ref[0, pl.ds(ws, win), cols]) / l
        o_ref[0, :, cols] = jnp.where(first, o[:tq], o[tq:]).astype(o_ref.dtype)

    s_next = scores(0)
    for p in range(pairs):
        s_cur = s_next
        if p + 1 < pairs:
            s_next = scores(p + 1)
        attend(p, s_cur)


def _attn_a(qkv, bias):
    b, s, _ = qkv.shape
    tq = ATTN_A_TQ
    win = tq + LEFT
    gw = ATTN_A_HEADS * HEAD_W // 2
    n_groups = D_MODEL // gw
    assert tq % HEAD_W == 0 and bias.shape[1:] == ((win + LEFT) // HEAD_W, 2 * tq, HEAD_W)
    return pl.pallas_call(
        functools.partial(_attn_a_kernel, tq=tq, win=win),
        grid=(b, n_groups, s // tq),
        in_specs=[
            pl.BlockSpec((1, tq, gw), lambda bi, hg, t: (bi, t, hg)),
            pl.BlockSpec((1, s, gw), lambda bi, hg, t: (bi, 0, n_groups + hg)),
            pl.BlockSpec((1, s, gw), lambda bi, hg, t: (bi, 0, 2 * n_groups + hg)),
            pl.BlockSpec((ATTN_A_HEADS // 2,) + bias.shape[1:], lambda bi, hg, t: (hg, 0, 0, 0)),
        ],
        out_specs=pl.BlockSpec((1, tq, gw), lambda bi, hg, t: (bi, t, hg)),
        out_shape=jax.ShapeDtypeStruct((b, s, D_MODEL), BF16),
        compiler_params=_params(3),
        name="attn_a",
    )(qkv, qkv, qkv, bias)


def _gmlp_in_kernel(x_ref, g_ref, w_ref, lng_ref, lnb_ref, u_ref, v_ref, *, g_row):
    xn = _rms(x_ref[...], g_ref[g_row:g_row + 1, :]).astype(BF16)
    hw = _mm(xn, w_ref[...])
    act = 0.5 * hw * (1.0 + lax.erf(hw * math.sqrt(0.5)))
    u_ref[...] = act[:, :D_MODEL].astype(u_ref.dtype)
    v = act[:, D_MODEL:]
    mu = jnp.mean(v, axis=-1, keepdims=True)
    vc = v - mu
    var = jnp.mean(vc * vc, axis=-1, keepdims=True)
    v_ref[...] = (vc * lax.rsqrt(var + EPS) * lng_ref[...] + lnb_ref[...]).astype(v_ref.dtype)


def _gmlp_in(x, g, w, ln_g, ln_b, g_row):
    t, d = x.shape
    out = jax.ShapeDtypeStruct((t, d), BF16)
    row_spec = pl.BlockSpec((ROW_TILE, d), lambda i: (i, 0))
    return pl.pallas_call(
        functools.partial(_gmlp_in_kernel, g_row=g_row),
        grid=(t // ROW_TILE,),
        in_specs=[row_spec, _resident(g), _resident(w),
                  _resident(ln_g), _resident(ln_b)],
        out_specs=[row_spec, row_spec],
        out_shape=[out, out],
        compiler_params=_params(1),
        name="gmlp_in",
    )(x, *_operands(g, w, ln_g, ln_b))


def _gmlp_out_kernel(u_ref, v_ref, ws_ref, bs_ref, wo_ref, x_ref, g_ref, out_ref, y_ref, *, g_row):
    c = GMLP_CHUNK
    row = lax.broadcasted_iota(jnp.int32, (c, c), 0)
    col = lax.broadcasted_iota(jnp.int32, (c, c), 1)
    causal = row >= col
    for grp in range(GMLP_GROUPS):
        w = ws_ref[grp]
        w = jnp.where(causal, w, jnp.zeros_like(w))
        b = bs_ref[:, grp:grp + 1]
        lanes = slice(grp * c, (grp + 1) * c)
        for blk in range(ROW_TILE // c):
            rows = slice(blk * c, (blk + 1) * c)
            sv = _mm(w, v_ref[rows, lanes]) + b
            y_ref[rows, lanes] = (u_ref[rows, lanes].astype(F32) * sv).astype(y_ref.dtype)
    m = _mm(y_ref[...], wo_ref[...])
    out_ref[...] = x_ref[...] + _rms(m, g_ref[g_row:g_row + 1, :])


def _gmlp_out(u, v, w_s, b_s_t, w_o, x, g, g_row):
    t, d = x.shape
    row_spec = pl.BlockSpec((ROW_TILE, d), lambda i: (i, 0))
    return pl.pallas_call(
        functools.partial(_gmlp_out_kernel, g_row=g_row),
        grid=(t // ROW_TILE,),
        in_specs=[row_spec, row_spec, _resident(w_s), _resident(b_s_t),
                  _resident(w_o), row_spec, _resident(g)],
        out_specs=row_spec,
        out_shape=jax.ShapeDtypeStruct((t, d), F32),
        scratch_shapes=[pltpu.VMEM((ROW_TILE, d), BF16)],
        compiler_params=_params(1),
        name="gmlp_out",
    )(u, v, *_operands(w_s, b_s_t, w_o), x, *_operands(g))


def _norm_proj_c_kernel(x_ref, g_ref, wk_ref, wqv_ref, qt_ref, k_ref, vt_ref, *, g_row, tq):
    xn = _rms(x_ref[...], g_ref[g_row:g_row + 1, :]).astype(BF16)
    k_ref[...] = _mm(xn, wk_ref[...]).astype(k_ref.dtype)
    qv_t = _mm_nt(wqv_ref[...], xn)
    rows = xn.shape[0]
    q_t = (qv_t[:D_MODEL] * ((HEAD_W // 2) ** -0.5 * LOG2_E)).astype(BF16)
    for part in range(rows // tq):
        cols = slice(part * tq, (part + 1) * tq)
        qt_ref[0, :, part] = q_t[:, cols].reshape(HEADS_C, HEAD_W, tq)
    vt_ref[0, :, 0] = qv_t[D_MODEL:].astype(BF16).reshape(HEADS_C, HEAD_W, rows)


def _norm_proj_c(x, g, w_k, w_qv_t, g_row, batch):
    t, d = x.shape
    s = t // batch
    tq = tk = ATTN_C_TILE
    assert ROW_TILE == tk
    per_batch = s // ROW_TILE
    return pl.pallas_call(
        functools.partial(_norm_proj_c_kernel, g_row=g_row, tq=tq),
        grid=(t // ROW_TILE,),
        in_specs=[
            pl.BlockSpec((ROW_TILE, d), lambda i: (i, 0)),
            _resident(g),
            _resident(w_k),
            _resident(w_qv_t),
        ],
        out_specs=[
            pl.BlockSpec((1, HEADS_C, tk // tq, HEAD_W, tq),
                         lambda i: (i // per_batch, 0, i % per_batch, 0, 0)),
            pl.BlockSpec((ROW_TILE, d), lambda i: (i, 0)),
            pl.BlockSpec((1, HEADS_C, 1, HEAD_W, tk),
                         lambda i: (i // per_batch, 0, i % per_batch, 0, 0)),
        ],
        out_shape=[
            jax.ShapeDtypeStruct((batch, HEADS_C, s // tq, HEAD_W, tq), BF16),
            jax.ShapeDtypeStruct((t, d), BF16),
            jax.ShapeDtypeStruct((batch, HEADS_C, s // tk, HEAD_W, tk), BF16),
        ],
        compiler_params=_params(1),
        name="norm_proj_c",
    )(x, *_operands(g, w_k, w_qv_t))


def _attn_c_kernel(lam_ref, subg_ref, qt_ref, k_ref, vt_ref, o_ref,
                   qs_ref, kaug_ref, s_ref, p_ref, diag_ref, smax_ref, m_ref, l_ref, alpha_ref,
                   acc_ref,
                   *, tile, lambda_init):
    h = pl.program_id(1)
    t = pl.program_id(2)
    n = 2 * tile
    lanes = HEAD_W
    group = tile
    n_aug = 2 * len(LOG2_E_BF16)
    inv_slope = jnp.left_shift(1, jnp.full((1, lanes), h + 1, jnp.int32)).astype(F32)

    @pl.when(t == 0)
    def _():
        key = lax.broadcasted_iota(jnp.int32, (tile, lanes), 0)
        lane = lax.broadcasted_iota(jnp.int32, (tile, lanes), 1)
        lo = (key & 255).astype(F32) / inv_slope
        hi = (key & 256).astype(F32) / inv_slope
        kaug_ref[...] = jnp.where(lane < n_aug // 2, lo,
                                  jnp.where(lane < n_aug, hi, 0.0)).astype(BF16)
        row = lax.broadcasted_iota(jnp.int32, (HEAD_W, n), 0)
        consts = jnp.zeros((HEAD_W, n), F32)
        for i, c in enumerate(LOG2_E_BF16 + LOG2_E_BF16):
            consts = jnp.where(row == i, c, consts)
        qs_ref[HEAD_W:, :] = consts.astype(BF16)
        for c in range(tile // lanes):
            qry = lax.broadcasted_iota(jnp.int32, (tile, lanes), 1) + c * lanes
            future = jnp.minimum(2 * (qry - key), 0).astype(F32) / inv_slope * LOG2_E
            allowed = jnp.right_shift(key, 6) <= jnp.right_shift(qry, 6)
            diag_ref[:, c * lanes:(c + 1) * lanes] = jnp.where(allowed, future, NEG_INF)

    qt = qt_ref[0, 0, 0]
    first = lax.broadcasted_iota(jnp.int32, (HEAD_W, tile), 0) < HEAD_W // 2
    zero = jnp.zeros_like(qt)
    qs_ref[:HEAD_W, :tile] = jnp.where(first, qt, zero)
    qs_ref[:HEAD_W, tile:] = jnp.where(first, zero, qt)

    m_ref[...] = jnp.full(m_ref.shape, NEG_INF, F32)
    l_ref[...] = jnp.zeros(l_ref.shape, F32)
    acc_ref[...] = jnp.zeros(acc_ref.shape, F32)

    def issue_scores(j, slot):
        start = pl.multiple_of(j * tile, tile)
        keys = jnp.concatenate([k_ref[0, pl.ds(start, tile), :], kaug_ref[...]], axis=1)
        for g in range(n // group):
            gcols = slice(g * group, (g + 1) * group)
            s = _mm(keys, qs_ref[:, gcols])
            s_ref[slot, :, gcols] = s
            smax_ref[slot, :, gcols] = jnp.max(s, axis=0, keepdims=True)

    def softmax(slot, bias, shift):
        for c in range(n // lanes):
            cols = slice(c * lanes, (c + 1) * lanes)
            if bias is None:
                smax = smax_ref[slot, :, cols]
            else:
                sc = s_ref[slot, :, cols] + bias(c)
                s_ref[slot, :, cols] = sc
                smax = jnp.max(sc, axis=0, keepdims=True)
            m_old = m_ref[:, cols]
            m_new = jnp.maximum(m_old, smax + shift)
            alpha = jnp.exp2(m_old - m_new)
            e = jnp.exp2(s_ref[slot, :, cols] - (m_new - shift))
            l_ref[:, cols] = alpha * l_ref[:, cols] + jnp.sum(e, axis=0, keepdims=True)
            m_ref[:, cols] = m_new
            p_ref[slot, :, cols] = e.astype(BF16)
            alpha_ref[slot, :, cols] = alpha

    def values(j, slot):
        acc_ref[...] = alpha_ref[slot] * acc_ref[...] + _mm(vt_ref[0, 0, j], p_ref[slot])

    def past_shift(i):
        return (jnp.full((1, lanes), (i - t) * tile, jnp.int32).astype(F32) / inv_slope
                * sum(LOG2_E_BF16))

    def diag_bias(c):
        q0 = (c * lanes) % tile
        return diag_ref[:, q0:q0 + lanes]

    def step(i, slot, bias, shift, last):
        values(jnp.maximum(i - 1, 0), 1 - slot)
        if not last:
            issue_scores(i + 1, 1 - slot)
        softmax(slot, bias, shift)

    p_ref[1] = jnp.zeros((tile, n), BF16)
    alpha_ref[1] = jnp.ones((1, n), F32)
    issue_scores(0, 0)

    def two_past_blocks(u, carry):
        step(2 * u, 0, None, past_shift(2 * u), False)
        step(2 * u + 1, 1, None, past_shift(2 * u + 1), False)
        return carry

    lax.fori_loop(0, t // 2, two_past_blocks, 0)
    no_shift = jnp.zeros((1, lanes), F32)

    @pl.when(t % 2 == 0)
    def _():
        step(t, 0, diag_bias, no_shift, True)
        values(t, 0)

    @pl.when(t % 2 == 1)
    def _():
        step(t - 1, 0, None, past_shift(t - 1), False)
        step(t, 1, diag_bias, no_shift, True)
        values(t, 1)

    o_t = acc_ref[...] / l_ref[...]
    lam = lam_ref[...]
    lam_full = (jnp.exp(jnp.sum(lam[0:1] * lam[1:2], axis=-1, keepdims=True))
                - jnp.exp(jnp.sum(lam[2:3] * lam[3:4], axis=-1, keepdims=True)) + lambda_init)
    a = (o_t[:, :tile] - lam_full * o_t[:, tile:]).T
    o_ref[0] = (_rms(a, subg_ref[...]) * (1.0 - lambda_init)).astype(o_ref.dtype)


def _attn_c(q_t, k, v_t, lam, subln_g, lambda_init):
    b, s, _ = k.shape
    tile = ATTN_C_TILE
    n = 2 * tile
    return pl.pallas_call(
        functools.partial(_attn_c_kernel, tile=tile, lambda_init=lambda_init),
        grid=(b, HEADS_C, s // tile),
        in_specs=[
            _resident(lam),
            _resident(subln_g),
            pl.BlockSpec((1, 1, 1, HEAD_W, tile), lambda bi, h, t: (bi, h, t, 0, 0)),
            pl.BlockSpec((1, s, HEAD_W), lambda bi, h, t: (bi, 0, h)),
            pl.BlockSpec((1, 1, s // tile, HEAD_W, tile), lambda bi, h, t: (bi, h, 0, 0, 0)),
        ],
        out_specs=pl.BlockSpec((1, tile, HEAD_W), lambda bi, h, t: (bi, t, h)),
        out_shape=jax.ShapeDtypeStruct((b, s, D_MODEL), BF16),
        scratch_shapes=[
            pltpu.VMEM((2 * HEAD_W, n), BF16),
            pltpu.VMEM((tile, HEAD_W), BF16),
            pltpu.VMEM((2, tile, n), F32),
            pltpu.VMEM((2, tile, n), BF16),
            pltpu.VMEM((tile, tile), F32),
            pltpu.VMEM((2, 1, n), F32),
            pltpu.VMEM((1, n), F32),
            pltpu.VMEM((1, n), F32),
            pltpu.VMEM((2, 1, n), F32),
            pltpu.VMEM((HEAD_W, n), F32),
        ],
        compiler_params=_params(3),
        name="attn_c",
    )(*_operands(lam, subln_g), q_t, k, v_t)


def kernel(x, norm_g, ff1_w_in, ff1_w_out, ff2_w_in, ff2_w_out, a_w_qkv, a_rel_bias, a_w_o,
           b_w_in, b_ln_g, b_ln_b, b_w_s, b_b_s, b_w_o, c_w_qkv, c_lambda, c_subln_g, c_w_o):
    b, s, d = x.shape
    depth = norm_g.shape[0]
    ff1_w_in, ff1_w_out, ff2_w_in, ff2_w_out, a_w_qkv, a_w_o, b_w_in, b_w_s, b_w_o, c_w_o = (
        w.astype(BF16) for w in
        (ff1_w_in, ff1_w_out, ff2_w_in, ff2_w_out, a_w_qkv, a_w_o, b_w_in, b_w_s, b_w_o, c_w_o))
    b_ln_g, b_ln_b, c_subln_g = b_ln_g[:, None], b_ln_b[:, None], c_subln_g[:, None]
    b_b_s_t = jnp.swapaxes(b_b_s, 1, 2)
    xf = x.reshape(b * s, d)
    for i in range(depth):
        g = _Layer(norm_g, i)
        xf = _ffn(xf, g, _Layer(ff1_w_in, i), _Layer(ff1_w_out, i), 0)
        kind, j = i % N_MIXERS, i // N_MIXERS
        if kind == 0:
            qkv = _norm_proj(xf, g, _Layer(a_w_qkv, j), 2)
            bias = _attn_a_bias(a_rel_bias[j], ATTN_A_TQ, ATTN_A_TQ + LEFT)
            o = _attn_a(qkv.reshape(b, s, 3 * d), bias)
            xf = _proj_res(o.reshape(b * s, d), _Layer(a_w_o, j), xf, g, 3)
        elif kind == 1:
            u, v = _gmlp_in(xf, g, _Layer(b_w_in, j), _Layer(b_ln_g, j), _Layer(b_ln_b, j), 2)
            xf = _gmlp_out(u, v, _Layer(b_w_s, j), _Layer(b_b_s_t, j), _Layer(b_w_o, j), xf, g, 3)
        else:
            lambda_init = 0.8 - 0.6 * math.exp(-0.3 * i)
            w = c_w_qkv[j].astype(BF16)
            w_qv_t = jnp.concatenate([w[:, :d], w[:, 2 * d:]], axis=1).T
            q_t, k, v_t = _norm_proj_c(xf, g, w[:, d:2 * d], w_qv_t, 2, b)
            o = _attn_c(q_t, k.reshape(b, s, d), v_t, _Layer(c_lambda, j), _Layer(c_subln_g, j),
                        lambda_init)
            xf = _proj_res(o.reshape(b * s, d), _Layer(c_w_o, j), xf, g, 3)
        xf = _ffn(xf, g, _Layer(ff2_w_in, i), _Layer(ff2_w_out, i), 4)
    return xf.reshape(b, s, d)
```

```python
import functools
import math
from typing import NamedTuple

import jax
import jax.numpy as jnp
import numpy as np
from jax import lax
from jax.experimental import pallas as pl
from jax.experimental.pallas import tpu as pltpu

F32 = jnp.float32
BF16 = jnp.bfloat16

D_MODEL = 1024
D_FF = 2816
EPS = 1e-6
NEG_INF = -1e30
N_MIXERS = 3

CHUNK = 64
LEFT = 8 * CHUNK
HEADS_A = 16
REL_CLIP = 128
GMLP_CHUNK = 128
GMLP_GROUPS = 8
HEADS_C = 8
HEAD_W = 128
LOG2_E = math.log2(math.e)
LOG2_E_BF16 = (1.4453125, -0.00262451171875, 7.063150405883789e-06)

V7X_VMEM_LIMIT_BYTES = 56 * 1024 * 1024
V7X_VREG_BYTES = 8 * 128 * 4


def _pad_rows(dtype):
    return V7X_VREG_BYTES // (128 * jnp.dtype(dtype).itemsize)

ROW_TILE = 512
FFN_SUB = 256
ATTN_A_TQ = 128
ATTN_A_HEADS = 8
ATTN_C_TILE = 512


def _params(n_axes):
    return pltpu.CompilerParams(
        dimension_semantics=("arbitrary",) * n_axes,
        vmem_limit_bytes=V7X_VMEM_LIMIT_BYTES,
    )


class _Layer(NamedTuple):
    stack: jax.Array
    index: int


def _resident(p):
    if isinstance(p, _Layer):
        tail = p.stack.shape[1:]
        return pl.BlockSpec((None,) + tail, lambda *_: (p.index,) + (0,) * len(tail),
                            pipeline_mode=pl.Buffered(1))
    return pl.BlockSpec(p.shape, lambda *_: (0,) * p.ndim, pipeline_mode=pl.Buffered(1))


def _operands(*params):
    return [p.stack if isinstance(p, _Layer) else p for p in params]


def _rms(x, g):
    return x * lax.rsqrt(jnp.mean(x * x, axis=-1, keepdims=True) + EPS) * g


def _mm(a, b):
    return jnp.dot(a, b, preferred_element_type=F32)


def _mm_nt(a, b):
    return lax.dot_general(a, b, (((1,), (1,)), ((), ())), preferred_element_type=F32)


def _ffn_kernel(x_ref, g_ref, win_ref, wout_ref, o_ref, *, g_row):
    g = g_ref[...]
    halves = [slice(r * FFN_SUB, (r + 1) * FFN_SUB) for r in range(x_ref.shape[0] // FFN_SUB)]
    gus = [_mm(_rms(x_ref[rows, :], g[g_row:g_row + 1]).astype(BF16), win_ref[...])
           for rows in halves]
    ys = []
    for gu in gus:
        gate = gu[:, :D_FF]
        up = gu[:, D_FF:]
        h = (gate * (1.0 / (1.0 + jnp.exp(-gate))) * up).astype(BF16)
        ys.append(_mm(h, wout_ref[...]))
    for rows, y in zip(halves, ys):
        o_ref[rows, :] = x_ref[rows, :] + 0.5 * _rms(y, g[g_row + 1:g_row + 2])


def _ffn(x, g, w_in, w_out, g_row):
    t, d = x.shape
    return pl.pallas_call(
        functools.partial(_ffn_kernel, g_row=g_row),
        grid=(t // ROW_TILE,),
        in_specs=[
            pl.BlockSpec((ROW_TILE, d), lambda i: (i, 0)),
            _resident(g),
            _resident(w_in),
            _resident(w_out),
        ],
        out_specs=pl.BlockSpec((ROW_TILE, d), lambda i: (i, 0)),
        out_shape=jax.ShapeDtypeStruct((t, d), F32),
        compiler_params=_params(1),
        name="ffn",
    )(x, *_operands(g, w_in, w_out))


N_GROUPS = D_MODEL // HEAD_W


def _norm_proj_t_kernel(x_ref, g_ref, wk_ref, wqv_ref, qt_ref, k_ref, vt_ref, *, g_row, tq, tv):
    xn = _rms(x_ref[...], g_ref[g_row:g_row + 1, :]).astype(BF16)
    k_ref[...] = _mm(xn, wk_ref[...]).astype(k_ref.dtype)
    qv_t = _mm_nt(wqv_ref[...], xn)
    rows = xn.shape[0]
    q_t = (qv_t[:D_MODEL] * ((HEAD_W // 2) ** -0.5 * LOG2_E)).astype(BF16)
    v_t = qv_t[D_MODEL:].astype(BF16)
    for part in range(rows // tq):
        qt_ref[0, :, part] = q_t[:, part * tq:(part + 1) * tq].reshape(N_GROUPS, HEAD_W, tq)
    for part in range(rows // tv):
        vt_ref[0, :, part] = v_t[:, part * tv:(part + 1) * tv].reshape(N_GROUPS, HEAD_W, tv)


def _norm_proj_t(x, g, w_k, w_qv_t, g_row, batch, tq, tv):
    t, d = x.shape
    s = t // batch
    per_batch = s // ROW_TILE

    def tiles(width):
        return pl.BlockSpec((1, N_GROUPS, ROW_TILE // width, HEAD_W, width),
                            lambda i: (i // per_batch, 0, i % per_batch, 0, 0))

    return pl.pallas_call(
        functools.partial(_norm_proj_t_kernel, g_row=g_row, tq=tq, tv=tv),
        grid=(t // ROW_TILE,),
        in_specs=[
            pl.BlockSpec((ROW_TILE, d), lambda i: (i, 0)),
            _resident(g),
            _resident(w_k),
            _resident(w_qv_t),
        ],
        out_specs=[tiles(tq), pl.BlockSpec((ROW_TILE, d), lambda i: (i, 0)), tiles(tv)],
        out_shape=[
            jax.ShapeDtypeStruct((batch, N_GROUPS, s // tq, HEAD_W, tq), BF16),
            jax.ShapeDtypeStruct((t, d), BF16),
            jax.ShapeDtypeStruct((batch, N_GROUPS, s // tv, HEAD_W, tv), BF16),
        ],
        compiler_params=_params(1),
        name="norm_proj_t",
    )(x, *_operands(g, w_k, w_qv_t))


def _proj_res_kernel(o_ref, w_ref, x_ref, g_ref, out_ref, *, g_row):
    m = _mm(o_ref[...], w_ref[...])
    out_ref[...] = x_ref[...] + _rms(m, g_ref[g_row:g_row + 1, :])


def _proj_res(o, w, x, g, g_row):
    t, d = x.shape
    return pl.pallas_call(
        functools.partial(_proj_res_kernel, g_row=g_row),
        grid=(t // ROW_TILE,),
        in_specs=[
            pl.BlockSpec((ROW_TILE, d), lambda i: (i, 0)),
            _resident(w),
            pl.BlockSpec((ROW_TILE, d), lambda i: (i, 0)),
            _resident(g),
        ],
        out_specs=pl.BlockSpec((ROW_TILE, d), lambda i: (i, 0)),
        out_shape=jax.ShapeDtypeStruct((t, d), F32),
        compiler_params=_params(1),
        name="proj_res",
    )(o, *_operands(w), x, *_operands(g))


def _attn_a_bias(rel_bias, tq, win):
    heads = rel_bias.shape[0]
    tbl = rel_bias.astype(F32)
    f = jnp.concatenate([
        jnp.broadcast_to(tbl[:, 2 * REL_CLIP:], (heads, tq + LEFT - REL_CLIP)),
        tbl[:, 2 * REL_CLIP - 1:0:-1],
    ], axis=1)
    p = win + tq
    skew = jnp.tile(jnp.pad(f, ((0, 0), (0, 1))), (1, tq))[:, :tq * (p - 1)]
    bias = skew.reshape(heads, tq, p - 1)[:, :, tq - 1:tq - 1 + win]
    r = np.arange(tq)[:, None] // CHUNK
    j = np.arange(win)[None, :]
    valid = (j >= r * CHUNK) & (j < (r + LEFT // CHUNK + 1) * CHUNK)
    bias = jnp.where(valid, bias * LOG2_E, NEG_INF)
    bias = jnp.pad(bias, ((0, 0), (0, 0), (0, LEFT)), constant_values=NEG_INF)
    bias = bias.reshape(HEADS_A // 2, 2, tq, win + LEFT)
    return jnp.transpose(bias, (0, 3, 1, 2)).reshape(HEADS_A // 2, win + LEFT, 2 * tq)


def _attn_a_kernel(qt_ref, k_ref, vt_ref, bias_ref, o_ref, *, tq, win):
    t = pl.program_id(2)
    wb = jnp.maximum(t - LEFT // tq, 0)
    ws = pl.multiple_of(wb * tq, tq)
    shift = pl.multiple_of(jnp.maximum(LEFT - t * tq, 0), tq)
    first = lax.broadcasted_iota(jnp.int32, (HEAD_W, tq), 0) < HEAD_W // 2
    pairs = ATTN_A_HEADS // 2

    def scores(pp):
        zero = jnp.zeros((HEAD_W, tq), BF16)
        blocks = []
        for p in (2 * pp, 2 * pp + 1):
            qt = qt_ref[0, p, 0]
            row = [zero] * 4
            row[2 * (p % 2)] = jnp.where(first, qt, zero)
            row[2 * (p % 2) + 1] = jnp.where(first, zero, qt)
            blocks.append(jnp.concatenate(row, axis=1))
        qs = jnp.concatenate(blocks, axis=0)
        keys = k_ref[0, pl.ds(ws, win), 2 * pp * HEAD_W:(2 * pp + 2) * HEAD_W]
        bias = jnp.concatenate([bias_ref[2 * pp, pl.ds(shift, win), :],
                                bias_ref[2 * pp + 1, pl.ds(shift, win), :]], axis=1)
        return _mm(keys, qs) + bias

    def attend(p, s):
        m = jnp.max(s, axis=0, keepdims=True)
        e = jnp.exp2(s - m)
        l = jnp.sum(e, axis=0, keepdims=True)
        v_t = jnp.concatenate([vt_ref[0, p, wb + c] for c in range(win // tq)], axis=1)
        o_t = _mm(v_t, e.astype(BF16)) / l
        o_t = jnp.where(first, o_t[:, :tq], o_t[:, tq:])
        o_ref[0, :, p * HEAD_W:(p + 1) * HEAD_W] = o_t.T.astype(o_ref.dtype)

    s_next = scores(0)
    for pp in range(pairs // 2):
        s_cur = s_next
        if pp + 1 < pairs // 2:
            s_next = scores(pp + 1)
        attend(2 * pp, s_cur[:, :2 * tq])
        attend(2 * pp + 1, s_cur[:, 2 * tq:])


def _attn_a(q_t, k, v_t, bias):
    b, s, _ = k.shape
    tq = ATTN_A_TQ
    win = tq + LEFT
    pairs = ATTN_A_HEADS // 2
    gw = pairs * HEAD_W
    assert bias.shape == (HEADS_A // 2, win + LEFT, 2 * tq)
    return pl.pallas_call(
        functools.partial(_attn_a_kernel, tq=tq, win=win),
        grid=(b, D_MODEL // gw, s // tq),
        in_specs=[
            pl.BlockSpec((1, pairs, 1, HEAD_W, tq), lambda bi, hg, t: (bi, hg, t, 0, 0)),
            pl.BlockSpec((1, s, gw), lambda bi, hg, t: (bi, 0, hg)),
            pl.BlockSpec((1, pairs, s // tq, HEAD_W, tq), lambda bi, hg, t: (bi, hg, 0, 0, 0)),
            pl.BlockSpec((pairs, win + LEFT, 2 * tq), lambda bi, hg, t: (hg, 0, 0)),
        ],
        out_specs=pl.BlockSpec((1, tq, gw), lambda bi, hg, t: (bi, t, hg)),
        out_shape=jax.ShapeDtypeStruct((b, s, D_MODEL), BF16),
        compiler_params=_params(3),
        name="attn_a",
    )(q_t, k, v_t, bias)


def _gmlp_in_kernel(x_ref, g_ref, w_ref, lng_ref, lnb_ref, u_ref, v_ref, *, g_row):
    xn = _rms(x_ref[...], g_ref[g_row:g_row + 1, :]).astype(BF16)
    hw = _mm(xn, w_ref[...])
    act = 0.5 * hw * (1.0 + lax.erf(hw * math.sqrt(0.5)))
    u_ref[...] = act[:, :D_MODEL].astype(u_ref.dtype)
    v = act[:, D_MODEL:]
    mu = jnp.mean(v, axis=-1, keepdims=True)
    vc = v - mu
    var = jnp.mean(vc * vc, axis=-1, keepdims=True)
    v_ref[...] = (vc * lax.rsqrt(var + EPS) * lng_ref[...] + lnb_ref[...]).astype(v_ref.dtype)


def _gmlp_in(x, g, w, ln_g, ln_b, g_row):
    t, d = x.shape
    out = jax.ShapeDtypeStruct((t, d), BF16)
    row_spec = pl.BlockSpec((ROW_TILE, d), lambda i: (i, 0))
    return pl.pallas_call(
        functools.partial(_gmlp_in_kernel, g_row=g_row),
        grid=(t // ROW_TILE,),
        in_specs=[row_spec, _resident(g), _resident(w),
                  _resident(ln_g), _resident(ln_b)],
        out_specs=[row_spec, row_spec],
        out_shape=[out, out],
        compiler_params=_params(1),
        name="gmlp_in",
    )(x, *_operands(g, w, ln_g, ln_b))


def _gmlp_out_kernel(u_ref, v_ref, ws_ref, bs_ref, wo_ref, x_ref, g_ref, out_ref, y_ref, *, g_row):
    c = GMLP_CHUNK
    row = lax.broadcasted_iota(jnp.int32, (c, c), 0)
    col = lax.broadcasted_iota(jnp.int32, (c, c), 1)
    causal = row >= col
    for grp in range(GMLP_GROUPS):
        w = ws_ref[grp]
        w = jnp.where(causal, w, jnp.zeros_like(w))
        b = bs_ref[:, grp:grp + 1]
        lanes = slice(grp * c, (grp + 1) * c)
        for blk in range(ROW_TILE // c):
            rows = slice(blk * c, (blk + 1) * c)
            sv = _mm(w, v_ref[rows, lanes]) + b
            y_ref[rows, lanes] = (u_ref[rows, lanes].astype(F32) * sv).astype(y_ref.dtype)
    m = _mm(y_ref[...], wo_ref[...])
    out_ref[...] = x_ref[...] + _rms(m, g_ref[g_row:g_row + 1, :])


def _gmlp_out(u, v, w_s, b_s_t, w_o, x, g, g_row):
    t, d = x.shape
    row_spec = pl.BlockSpec((ROW_TILE, d), lambda i: (i, 0))
    return pl.pallas_call(
        functools.partial(_gmlp_out_kernel, g_row=g_row),
        grid=(t // ROW_TILE,),
        in_specs=[row_spec, row_spec, _resident(w_s), _resident(b_s_t),
                  _resident(w_o), row_spec, _resident(g)],
        out_specs=row_spec,
        out_shape=jax.ShapeDtypeStruct((t, d), F32),
        scratch_shapes=[pltpu.VMEM((ROW_TILE, d), BF16)],
        compiler_params=_params(1),
        name="gmlp_out",
    )(u, v, *_operands(w_s, b_s_t, w_o), x, *_operands(g))


def _attn_c_kernel(lam_ref, subg_ref, qt_ref, k_ref, vt_ref, o_ref,
                   qs_ref, kaug_ref, s_ref, p_ref, diag_ref, smax_ref, m_ref, l_ref, alpha_ref,
                   acc_ref,
                   *, tile, lambda_init):
    h = pl.program_id(1)
    t = pl.program_id(2)
    n = 2 * tile
    lanes = HEAD_W
    group = tile
    n_aug = 2 * len(LOG2_E_BF16)
    inv_slope = jnp.left_shift(1, jnp.full((1, lanes), h + 1, jnp.int32)).astype(F32)

    @pl.when(t == 0)
    def _():
        key = lax.broadcasted_iota(jnp.int32, (tile, lanes), 0)
        lane = lax.broadcasted_iota(jnp.int32, (tile, lanes), 1)
        lo = (key & 255).astype(F32) / inv_slope
        hi = (key & 256).astype(F32) / inv_slope
        kaug_ref[...] = jnp.where(lane < n_aug // 2, lo,
                                  jnp.where(lane < n_aug, hi, 0.0)).astype(BF16)
        row = lax.broadcasted_iota(jnp.int32, (HEAD_W, n), 0)
        consts = jnp.zeros((HEAD_W, n), F32)
        for i, c in enumerate(LOG2_E_BF16 + LOG2_E_BF16):
            consts = jnp.where(row == i, c, consts)
        qs_ref[HEAD_W:, :] = consts.astype(BF16)
        for c in range(tile // lanes):
            qry = lax.broadcasted_iota(jnp.int32, (tile, lanes), 1) + c * lanes
            future = jnp.minimum(2 * (qry - key), 0).astype(F32) / inv_slope * LOG2_E
            allowed = jnp.right_shift(key, 6) <= jnp.right_shift(qry, 6)
            diag_ref[c, :tile, :] = jnp.where(allowed, future, NEG_INF)

    qt = qt_ref[0, 0, 0]
    first = lax.broadcasted_iota(jnp.int32, (HEAD_W, tile), 0) < HEAD_W // 2
    zero = jnp.zeros_like(qt)
    qs_ref[:HEAD_W, :tile] = jnp.where(first, qt, zero)
    qs_ref[:HEAD_W, tile:] = jnp.where(first, zero, qt)

    m_ref[...] = jnp.full(m_ref.shape, NEG_INF, F32)
    l_ref[...] = jnp.zeros(l_ref.shape, F32)
    acc_ref[...] = jnp.zeros(acc_ref.shape, F32)

    def issue_scores(j, slot):
        start = pl.multiple_of(j * tile, tile)
        keys = jnp.concatenate([k_ref[0, pl.ds(start, tile), :], kaug_ref[...]], axis=1)
        for g in range(n // group):
            gcols = slice(g * group, (g + 1) * group)
            s = _mm(keys, qs_ref[:, gcols])
            for c in range(group // lanes):
                s_ref[slot, g * (group // lanes) + c, :tile, :] = s[:, c * lanes:(c + 1) * lanes]
            smax_ref[slot, :, gcols] = jnp.max(s, axis=0, keepdims=True)

    def softmax(slot, bias, shift):
        for c in range(n // lanes):
            cols = slice(c * lanes, (c + 1) * lanes)
            if bias is None:
                smax = smax_ref[slot, :, cols]
            else:
                sc = s_ref[slot, c, :tile, :] + bias(c)
                s_ref[slot, c, :tile, :] = sc
                smax = jnp.max(sc, axis=0, keepdims=True)
            m_old = m_ref[:, cols]
            m_new = jnp.maximum(m_old, smax + shift)
            alpha = jnp.exp2(m_old - m_new)
            e = jnp.exp2(s_ref[slot, c, :tile, :] - (m_new - shift))
            l_ref[:, cols] = alpha * l_ref[:, cols] + jnp.sum(e, axis=0, keepdims=True)
            m_ref[:, cols] = m_new
            p_ref[slot, c, :tile, :] = e.astype(BF16)
            alpha_ref[slot, :, cols] = alpha

    def values(j, slot):
        p = jnp.concatenate([p_ref[slot, c, :tile, :] for c in range(n // lanes)], axis=1)
        acc_ref[...] = alpha_ref[slot] * acc_ref[...] + _mm(vt_ref[0, 0, j], p)

    def past_shift(i):
        return (jnp.full((1, lanes), (i - t) * tile, jnp.int32).astype(F32) / inv_slope
                * sum(LOG2_E_BF16))

    def diag_bias(c):
        return diag_ref[c % (tile // lanes), :tile, :]

    def step(i, slot, bias, shift, last):
        values(jnp.maximum(i - 1, 0), 1 - slot)
        if not last:
            issue_scores(i + 1, 1 - slot)
        softmax(slot, bias, shift)

    p_ref[1] = jnp.zeros(p_ref.shape[1:], BF16)
    alpha_ref[1] = jnp.ones((1, n), F32)
    issue_scores(0, 0)

    def two_past_blocks(u, carry):
        step(2 * u, 0, None, past_shift(2 * u), False)
        step(2 * u + 1, 1, None, past_shift(2 * u + 1), False)
        return carry

    lax.fori_loop(0, t // 2, two_past_blocks, 0)
    no_shift = jnp.zeros((1, lanes), F32)

    @pl.when(t % 2 == 0)
    def _():
        step(t, 0, diag_bias, no_shift, True)
        values(t, 0)

    @pl.when(t % 2 == 1)
    def _():
        step(t - 1, 0, None, past_shift(t - 1), False)
        step(t, 1, diag_bias, no_shift, True)
        values(t, 1)

    o_t = acc_ref[...] / l_ref[...]
    lam = lam_ref[...]
    lam_full = (jnp.exp(jnp.sum(lam[0:1] * lam[1:2], axis=-1, keepdims=True))
                - jnp.exp(jnp.sum(lam[2:3] * lam[3:4], axis=-1, keepdims=True)) + lambda_init)
    a = (o_t[:, :tile] - lam_full * o_t[:, tile:]).T
    o_ref[0] = (_rms(a, subg_ref[...]) * (1.0 - lambda_init)).astype(o_ref.dtype)


def _attn_c(q_t, k, v_t, lam, subln_g, lambda_init):
    b, s, _ = k.shape
    tile = ATTN_C_TILE
    n = 2 * tile
    return pl.pallas_call(
        functools.partial(_attn_c_kernel, tile=tile, lambda_init=lambda_init),
        grid=(b, HEADS_C, s // tile),
        in_specs=[
            _resident(lam),
            _resident(subln_g),
            pl.BlockSpec((1, 1, 1, HEAD_W, tile), lambda bi, h, t: (bi, h, t, 0, 0)),
            pl.BlockSpec((1, s, HEAD_W), lambda bi, h, t: (bi, 0, h)),
            pl.BlockSpec((1, 1, s // tile, HEAD_W, tile), lambda bi, h, t: (bi, h, 0, 0, 0)),
        ],
        out_specs=pl.BlockSpec((1, tile, HEAD_W), lambda bi, h, t: (bi, t, h)),
        out_shape=jax.ShapeDtypeStruct((b, s, D_MODEL), BF16),
        scratch_shapes=[
            pltpu.VMEM((2 * HEAD_W, n), BF16),
            pltpu.VMEM((tile, HEAD_W), BF16),
            pltpu.VMEM((2, n // HEAD_W, tile + _pad_rows(F32), HEAD_W), F32),
            pltpu.VMEM((2, n // HEAD_W, tile + _pad_rows(BF16), HEAD_W), BF16),
            pltpu.VMEM((tile // HEAD_W, tile + _pad_rows(F32), HEAD_W), F32),
            pltpu.VMEM((2, 1, n), F32),
            pltpu.VMEM((1, n), F32),
            pltpu.VMEM((1, n), F32),
            pltpu.VMEM((2, 1, n), F32),
            pltpu.VMEM((HEAD_W, n), F32),
        ],
        compiler_params=_params(3),
        name="attn_c",
    )(*_operands(lam, subln_g), q_t, k, v_t)


def kernel(x, norm_g, ff1_w_in, ff1_w_out, ff2_w_in, ff2_w_out, a_w_qkv, a_rel_bias, a_w_o,
           b_w_in, b_ln_g, b_ln_b, b_w_s, b_b_s, b_w_o, c_w_qkv, c_lambda, c_subln_g, c_w_o):
    b, s, d = x.shape
    depth = norm_g.shape[0]
    ff1_w_in, ff1_w_out, ff2_w_in, ff2_w_out, a_w_o, b_w_in, b_w_s, b_w_o, c_w_o = (
        w.astype(BF16) for w in
        (ff1_w_in, ff1_w_out, ff2_w_in, ff2_w_out, a_w_o, b_w_in, b_w_s, b_w_o, c_w_o))

    def split_qkv(w):
        w = w.astype(BF16)
        return w[:, d:2 * d], jnp.concatenate([w[:, :d], w[:, 2 * d:]], axis=1).T

    b_ln_g, b_ln_b, c_subln_g = b_ln_g[:, None], b_ln_b[:, None], c_subln_g[:, None]
    b_b_s_t = jnp.swapaxes(b_b_s, 1, 2)
    xf = x.reshape(b * s, d)
    for i in range(depth):
        g = _Layer(norm_g, i)
        xf = _ffn(xf, g, _Layer(ff1_w_in, i), _Layer(ff1_w_out, i), 0)
        kind, j = i % N_MIXERS, i // N_MIXERS
        if kind == 0:
            w_k, w_qv_t = split_qkv(a_w_qkv[j])
            q_t, k, v_t = _norm_proj_t(xf, g, w_k, w_qv_t, 2, b, ATTN_A_TQ, ATTN_A_TQ)
            bias = _attn_a_bias(a_rel_bias[j], ATTN_A_TQ, ATTN_A_TQ + LEFT)
            o = _attn_a(q_t, k.reshape(b, s, d), v_t, bias)
            xf = _proj_res(o.reshape(b * s, d), _Layer(a_w_o, j), xf, g, 3)
        elif kind == 1:
            u, v = _gmlp_in(xf, g, _Layer(b_w_in, j), _Layer(b_ln_g, j), _Layer(b_ln_b, j), 2)
            xf = _gmlp_out(u, v, _Layer(b_w_s, j), _Layer(b_b_s_t, j), _Layer(b_w_o, j), xf, g, 3)
        else:
            lambda_init = 0.8 - 0.6 * math.exp(-0.3 * i)
            w_k, w_qv_t = split_qkv(c_w_qkv[j])
            q_t, k, v_t = _norm_proj_t(xf, g, w_k, w_qv_t, 2, b, ATTN_C_TILE, ATTN_C_TILE)
            o = _attn_c(q_t, k.reshape(b, s, d), v_t, _Layer(c_lambda, j), _Layer(c_subln_g, j),
                        lambda_init)
            xf = _proj_res(o.reshape(b * s, d), _Layer(c_w_o, j), xf, g, 3)
        xf = _ffn(xf, g, _Layer(ff2_w_in, i), _Layer(ff2_w_out, i), 4)
    return xf.reshape(b, s, d)
```

```python
import functools
import math
from typing import NamedTuple

import jax
import jax.numpy as jnp
import numpy as np
from jax import lax
from jax.experimental import pallas as pl
from jax.experimental.pallas import tpu as pltpu

F32 = jnp.float32
BF16 = jnp.bfloat16

D_MODEL = 1024
D_FF = 2816
EPS = 1e-6
NEG_INF = -1e30
N_MIXERS = 3

CHUNK = 64
LEFT = 8 * CHUNK
HEADS_A = 16
REL_CLIP = 128
GMLP_CHUNK = 128
GMLP_GROUPS = 8
HEADS_C = 8
HEAD_W = 128
LOG2_E = math.log2(math.e)
LOG2_E_BF16 = (1.4453125, -0.00262451171875, 7.063150405883789e-06)

V7X_VMEM_LIMIT_BYTES = 56 * 1024 * 1024
V7X_VREG_BYTES = 8 * 128 * 4


def _pad_rows(dtype):
    return V7X_VREG_BYTES // (128 * jnp.dtype(dtype).itemsize)

ROW_TILE = 512
FFN_ROW_TILE = 1024
FFN_SUB = 256
ATTN_A_TQ = 128
ATTN_A_HEADS = 8
ATTN_C_TILE = 512


def _params(n_axes):
    return pltpu.CompilerParams(
        dimension_semantics=("arbitrary",) * n_axes,
        vmem_limit_bytes=V7X_VMEM_LIMIT_BYTES,
    )


class _Layer(NamedTuple):
    stack: jax.Array
    index: int


def _resident(p):
    if isinstance(p, _Layer):
        tail = p.stack.shape[1:]
        return pl.BlockSpec((None,) + tail, lambda *_: (p.index,) + (0,) * len(tail),
                            pipeline_mode=pl.Buffered(1))
    return pl.BlockSpec(p.shape, lambda *_: (0,) * p.ndim, pipeline_mode=pl.Buffered(1))


def _operands(*params):
    return [p.stack if isinstance(p, _Layer) else p for p in params]


def _rms(x, g):
    return x * lax.rsqrt(jnp.mean(x * x, axis=-1, keepdims=True) + EPS) * g


def _mm(a, b):
    return jnp.dot(a, b, preferred_element_type=F32)


def _mm_nt(a, b):
    return lax.dot_general(a, b, (((1,), (1,)), ((), ())), preferred_element_type=F32)


def _ffn_kernel(x_ref, g_ref, win_ref, wout_ref, o_ref, *, g_row):
    g = g_ref[...]
    chains = [slice(r * FFN_SUB, (r + 1) * FFN_SUB) for r in range(x_ref.shape[0] // FFN_SUB)]

    def up_proj(rows):
        return _mm(_rms(x_ref[rows, :], g[g_row:g_row + 1]).astype(BF16), win_ref[...])

    def down_proj(gu):
        gate = gu[:, :D_FF]
        up = gu[:, D_FF:]
        h = (gate * (1.0 / (1.0 + jnp.exp(-gate))) * up).astype(BF16)
        return _mm(h, wout_ref[...])

    gu_next = up_proj(chains[0])
    y_prev = None
    for r in range(len(chains)):
        gu = gu_next
        if r + 1 < len(chains):
            gu_next = up_proj(chains[r + 1])
        y = down_proj(gu)
        if y_prev is not None:
            o_ref[chains[r - 1], :] = (x_ref[chains[r - 1], :]
                                       + 0.5 * _rms(y_prev, g[g_row + 1:g_row + 2]))
        y_prev = y
    o_ref[chains[-1], :] = x_ref[chains[-1], :] + 0.5 * _rms(y_prev, g[g_row + 1:g_row + 2])


def _ffn(x, g, w_in, w_out, g_row):
    t, d = x.shape
    return pl.pallas_call(
        functools.partial(_ffn_kernel, g_row=g_row),
        grid=(t // FFN_ROW_TILE,),
        in_specs=[
            pl.BlockSpec((FFN_ROW_TILE, d), lambda i: (i, 0)),
            _resident(g),
            _resident(w_in),
            _resident(w_out),
        ],
        out_specs=pl.BlockSpec((FFN_ROW_TILE, d), lambda i: (i, 0)),
        out_shape=jax.ShapeDtypeStruct((t, d), F32),
        compiler_params=_params(1),
        name="ffn",
    )(x, *_operands(g, w_in, w_out))


N_GROUPS = D_MODEL // HEAD_W


def _norm_proj_t_kernel(x_ref, g_ref, wk_ref, wqv_ref, qt_ref, k_ref, vt_ref, *, g_row, tq, tv):
    xn = _rms(x_ref[...], g_ref[g_row:g_row + 1, :]).astype(BF16)
    k_ref[...] = _mm(xn, wk_ref[...]).astype(k_ref.dtype)
    qv_t = _mm_nt(wqv_ref[...], xn)
    rows = xn.shape[0]
    q_t = (qv_t[:D_MODEL] * ((HEAD_W // 2) ** -0.5 * LOG2_E)).astype(BF16)
    v_t = qv_t[D_MODEL:].astype(BF16)
    for part in range(rows // tq):
        qt_ref[0, :, part] = q_t[:, part * tq:(part + 1) * tq].reshape(N_GROUPS, HEAD_W, tq)
    for part in range(rows // tv):
        vt_ref[0, :, part] = v_t[:, part * tv:(part + 1) * tv].reshape(N_GROUPS, HEAD_W, tv)


def _norm_proj_t(x, g, w_k, w_qv_t, g_row, batch, tq, tv):
    t, d = x.shape
    s = t // batch
    per_batch = s // ROW_TILE

    def tiles(width):
        return pl.BlockSpec((1, N_GROUPS, ROW_TILE // width, HEAD_W, width),
                            lambda i: (i // per_batch, 0, i % per_batch, 0, 0))

    return pl.pallas_call(
        functools.partial(_norm_proj_t_kernel, g_row=g_row, tq=tq, tv=tv),
        grid=(t // ROW_TILE,),
        in_specs=[
            pl.BlockSpec((ROW_TILE, d), lambda i: (i, 0)),
            _resident(g),
            _resident(w_k),
            _resident(w_qv_t),
        ],
        out_specs=[tiles(tq), pl.BlockSpec((ROW_TILE, d), lambda i: (i, 0)), tiles(tv)],
        out_shape=[
            jax.ShapeDtypeStruct((batch, N_GROUPS, s // tq, HEAD_W, tq), BF16),
            jax.ShapeDtypeStruct((t, d), BF16),
            jax.ShapeDtypeStruct((batch, N_GROUPS, s // tv, HEAD_W, tv), BF16),
        ],
        compiler_params=_params(1),
        name="norm_proj_t",
    )(x, *_operands(g, w_k, w_qv_t))


def _proj_res_kernel(o_ref, w_ref, x_ref, g_ref, out_ref, *, g_row):
    m = _mm(o_ref[...], w_ref[...])
    out_ref[...] = x_ref[...] + _rms(m, g_ref[g_row:g_row + 1, :])


def _proj_res(o, w, x, g, g_row):
    t, d = x.shape
    return pl.pallas_call(
        functools.partial(_proj_res_kernel, g_row=g_row),
        grid=(t // ROW_TILE,),
        in_specs=[
            pl.BlockSpec((ROW_TILE, d), lambda i: (i, 0)),
            _resident(w),
            pl.BlockSpec((ROW_TILE, d), lambda i: (i, 0)),
            _resident(g),
        ],
        out_specs=pl.BlockSpec((ROW_TILE, d), lambda i: (i, 0)),
        out_shape=jax.ShapeDtypeStruct((t, d), F32),
        compiler_params=_params(1),
        name="proj_res",
    )(o, *_operands(w), x, *_operands(g))


def _attn_a_bias(rel_bias, tq, win):
    heads = rel_bias.shape[0]
    tbl = rel_bias.astype(F32)
    f = jnp.concatenate([
        jnp.broadcast_to(tbl[:, 2 * REL_CLIP:], (heads, tq + LEFT - REL_CLIP)),
        tbl[:, 2 * REL_CLIP - 1:0:-1],
    ], axis=1)
    p = win + tq
    skew = jnp.tile(jnp.pad(f, ((0, 0), (0, 1))), (1, tq))[:, :tq * (p - 1)]
    bias = skew.reshape(heads, tq, p - 1)[:, :, tq - 1:tq - 1 + win]
    r = np.arange(tq)[:, None] // CHUNK
    j = np.arange(win)[None, :]
    valid = (j >= r * CHUNK) & (j < (r + LEFT // CHUNK + 1) * CHUNK)
    bias = jnp.where(valid, bias * LOG2_E, NEG_INF)
    bias = jnp.pad(bias, ((0, 0), (0, 0), (0, LEFT + _pad_rows(F32))), constant_values=NEG_INF)
    return jnp.swapaxes(bias, 1, 2)


def _attn_a_kernel(qt_ref, k_ref, vt_ref, bias_ref, o_ref, s_ref, p_ref, smax_ref, *, tq, win):
    t = pl.program_id(2)
    wb = jnp.maximum(t - LEFT // tq, 0)
    ws = pl.multiple_of(wb * tq, tq)
    shift = pl.multiple_of(jnp.maximum(LEFT - t * tq, 0), tq)
    first = lax.broadcasted_iota(jnp.int32, (HEAD_W, tq), 0) < HEAD_W // 2

    def scores(pp):
        zero = jnp.zeros((HEAD_W, tq), BF16)
        blocks = []
        for p in (2 * pp, 2 * pp + 1):
            qt = qt_ref[0, p, 0]
            row = [zero] * 4
            row[2 * (p % 2)] = jnp.where(first, qt, zero)
            row[2 * (p % 2) + 1] = jnp.where(first, zero, qt)
            blocks.append(jnp.concatenate(row, axis=1))
        qs = jnp.concatenate(blocks, axis=0)
        keys = k_ref[0, pl.ds(ws, win), 2 * pp * HEAD_W:(2 * pp + 2) * HEAD_W]
        s = _mm(keys, qs)
        for c in range(4):
            h = 4 * pp + c
            sc = s[:, c * tq:(c + 1) * tq] + bias_ref[h, pl.ds(shift, win), :]
            s_ref[h, :win, :] = sc
            smax_ref[:, h * tq:(h + 1) * tq] = jnp.max(sc, axis=0, keepdims=True)

    def attend(p):
        sums = []
        for h in (2 * p, 2 * p + 1):
            e = jnp.exp2(s_ref[h, :win, :] - smax_ref[:, h * tq:(h + 1) * tq])
            sums.append(jnp.sum(e, axis=0, keepdims=True))
            p_ref[h, :win, :] = e.astype(BF16)
        probs = jnp.concatenate([p_ref[2 * p, :win, :], p_ref[2 * p + 1, :win, :]], axis=1)
        v_t = jnp.concatenate([vt_ref[0, p, wb + c] for c in range(win // tq)], axis=1)
        o_t = _mm(v_t, probs) / jnp.concatenate(sums, axis=1)
        o_t = jnp.where(first, o_t[:, :tq], o_t[:, tq:])
        o_ref[0, :, p * HEAD_W:(p + 1) * HEAD_W] = o_t.T.astype(o_ref.dtype)

    quads = ATTN_A_HEADS // 4
    scores(0)
    for pp in range(quads):
        if pp + 1 < quads:
            scores(pp + 1)
        attend(2 * pp)
        attend(2 * pp + 1)


def _attn_a(q_t, k, v_t, bias):
    b, s, _ = k.shape
    tq = ATTN_A_TQ
    win = tq + LEFT
    heads = ATTN_A_HEADS
    gw = heads // 2 * HEAD_W
    assert tq == HEAD_W and bias.shape[0] == HEADS_A and bias.shape[2] == tq
    return pl.pallas_call(
        functools.partial(_attn_a_kernel, tq=tq, win=win),
        grid=(b, D_MODEL // gw, s // tq),
        in_specs=[
            pl.BlockSpec((1, heads // 2, 1, HEAD_W, tq), lambda bi, hg, t: (bi, hg, t, 0, 0)),
            pl.BlockSpec((1, s, gw), lambda bi, hg, t: (bi, 0, hg)),
            pl.BlockSpec((1, heads // 2, s // tq, HEAD_W, tq), lambda bi, hg, t: (bi, hg, 0, 0, 0)),
            pl.BlockSpec((heads,) + bias.shape[1:], lambda bi, hg, t: (hg, 0, 0)),
        ],
        out_specs=pl.BlockSpec((1, tq, gw), lambda bi, hg, t: (bi, t, hg)),
        out_shape=jax.ShapeDtypeStruct((b, s, D_MODEL), BF16),
        scratch_shapes=[
            pltpu.VMEM((heads, win + _pad_rows(F32), tq), F32),
            pltpu.VMEM((heads, win + _pad_rows(BF16), tq), BF16),
            pltpu.VMEM((1, heads * tq), F32),
        ],
        compiler_params=_params(3),
        name="attn_a",
    )(q_t, k, v_t, bias)


def _gmlp_in_kernel(x_ref, g_ref, w_ref, lng_ref, lnb_ref, u_ref, v_ref, *, g_row):
    xn = _rms(x_ref[...], g_ref[g_row:g_row + 1, :]).astype(BF16)
    hw = _mm(xn, w_ref[...])
    act = 0.5 * hw * (1.0 + lax.erf(hw * math.sqrt(0.5)))
    u_ref[...] = act[:, :D_MODEL].astype(u_ref.dtype)
    v = act[:, D_MODEL:]
    mu = jnp.mean(v, axis=-1, keepdims=True)
    vc = v - mu
    var = jnp.mean(vc * vc, axis=-1, keepdims=True)
    v_ref[...] = (vc * lax.rsqrt(var + EPS) * lng_ref[...] + lnb_ref[...]).astype(v_ref.dtype)


def _gmlp_in(x, g, w, ln_g, ln_b, g_row):
    t, d = x.shape
    out = jax.ShapeDtypeStruct((t, d), BF16)
    row_spec = pl.BlockSpec((ROW_TILE, d), lambda i: (i, 0))
    return pl.pallas_call(
        functools.partial(_gmlp_in_kernel, g_row=g_row),
        grid=(t // ROW_TILE,),
        in_specs=[row_spec, _resident(g), _resident(w),
                  _resident(ln_g), _resident(ln_b)],
        out_specs=[row_spec, row_spec],
        out_shape=[out, out],
        compiler_params=_params(1),
        name="gmlp_in",
    )(x, *_operands(g, w, ln_g, ln_b))


def _gmlp_out_kernel(u_ref, v_ref, ws_ref, bs_ref, wo_ref, x_ref, g_ref, out_ref, y_ref, *, g_row):
    c = GMLP_CHUNK
    row = lax.broadcasted_iota(jnp.int32, (c, c), 0)
    col = lax.broadcasted_iota(jnp.int32, (c, c), 1)
    causal = row >= col
    for grp in range(GMLP_GROUPS):
        w = ws_ref[grp]
        w = jnp.where(causal, w, jnp.zeros_like(w))
        b = bs_ref[:, grp:grp + 1]
        lanes = slice(grp * c, (grp + 1) * c)
        for blk in range(ROW_TILE // c):
            rows = slice(blk * c, (blk + 1) * c)
            sv = _mm(w, v_ref[rows, lanes]) + b
            y_ref[rows, lanes] = (u_ref[rows, lanes].astype(F32) * sv).astype(y_ref.dtype)
    m = _mm(y_ref[...], wo_ref[...])
    out_ref[...] = x_ref[...] + _rms(m, g_ref[g_row:g_row + 1, :])


def _gmlp_out(u, v, w_s, b_s_t, w_o, x, g, g_row):
    t, d = x.shape
    row_spec = pl.BlockSpec((ROW_TILE, d), lambda i: (i, 0))
    return pl.pallas_call(
        functools.partial(_gmlp_out_kernel, g_row=g_row),
        grid=(t // ROW_TILE,),
        in_specs=[row_spec, row_spec, _resident(w_s), _resident(b_s_t),
                  _resident(w_o), row_spec, _resident(g)],
        out_specs=row_spec,
        out_shape=jax.ShapeDtypeStruct((t, d), F32),
        scratch_shapes=[pltpu.VMEM((ROW_TILE, d), BF16)],
        compiler_params=_params(1),
        name="gmlp_out",
    )(u, v, *_operands(w_s, b_s_t, w_o), x, *_operands(g))


def _attn_c_kernel(lam_ref, subg_ref, qt_ref, k_ref, vt_ref, o_ref,
                   qs_ref, kaug_ref, s_ref, p_ref, diag_ref, smax_ref, m_ref, l_ref, alpha_ref,
                   acc_ref,
                   *, tile, lambda_init):
    h = pl.program_id(1)
    t = pl.program_id(2)
    n = 2 * tile
    lanes = HEAD_W
    group = tile
    n_aug = 2 * len(LOG2_E_BF16)
    inv_slope = jnp.left_shift(1, jnp.full((1, lanes), h + 1, jnp.int32)).astype(F32)

    @pl.when(t == 0)
    def _():
        key = lax.broadcasted_iota(jnp.int32, (tile, lanes), 0)
        lane = lax.broadcasted_iota(jnp.int32, (tile, lanes), 1)
        lo = (key & 255).astype(F32) / inv_slope
        hi = (key & 256).astype(F32) / inv_slope
        kaug_ref[...] = jnp.where(lane < n_aug // 2, lo,
                                  jnp.where(lane < n_aug, hi, 0.0)).astype(BF16)
        row = lax.broadcasted_iota(jnp.int32, (HEAD_W, n), 0)
        consts = jnp.zeros((HEAD_W, n), F32)
        for i, c in enumerate(LOG2_E_BF16 + LOG2_E_BF16):
            consts = jnp.where(row == i, c, consts)
        qs_ref[HEAD_W:, :] = consts.astype(BF16)
        for c in range(tile // lanes):
            qry = lax.broadcasted_iota(jnp.int32, (tile, lanes), 1) + c * lanes
            future = jnp.minimum(2 * (qry - key), 0).astype(F32) / inv_slope * LOG2_E
            allowed = jnp.right_shift(key, 6) <= jnp.right_shift(qry, 6)
            diag_ref[c, :tile, :] = jnp.where(allowed, future, NEG_INF)

    qt = qt_ref[0, 0, 0]
    first = lax.broadcasted_iota(jnp.int32, (HEAD_W, tile), 0) < HEAD_W // 2
    zero = jnp.zeros_like(qt)
    qs_ref[:HEAD_W, :tile] = jnp.where(first, qt, zero)
    qs_ref[:HEAD_W, tile:] = jnp.where(first, zero, qt)

    m_ref[...] = jnp.full(m_ref.shape, NEG_INF, F32)
    l_ref[...] = jnp.zeros(l_ref.shape, F32)
    acc_ref[...] = jnp.zeros(acc_ref.shape, F32)

    def issue_scores(j, slot):
        start = pl.multiple_of(j * tile, tile)
        keys = jnp.concatenate([k_ref[0, pl.ds(start, tile), :], kaug_ref[...]], axis=1)
        for g in range(n // group):
            gcols = slice(g * group, (g + 1) * group)
            s = _mm(keys, qs_ref[:, gcols])
            for c in range(group // lanes):
                s_ref[slot, g * (group // lanes) + c, :tile, :] = s[:, c * lanes:(c + 1) * lanes]
            smax_ref[slot, :, gcols] = jnp.max(s, axis=0, keepdims=True)

    def softmax(slot, bias, shift):
        for c in range(n // lanes):
            cols = slice(c * lanes, (c + 1) * lanes)
            if bias is None:
                smax = smax_ref[slot, :, cols]
            else:
                sc = s_ref[slot, c, :tile, :] + bias(c)
                s_ref[slot, c, :tile, :] = sc
                smax = jnp.max(sc, axis=0, keepdims=True)
            m_old = m_ref[:, cols]
            m_new = jnp.maximum(m_old, smax + shift)
            alpha = jnp.exp2(m_old - m_new)
            e = jnp.exp2(s_ref[slot, c, :tile, :] - (m_new - shift))
            l_ref[:, cols] = alpha * l_ref[:, cols] + jnp.sum(e, axis=0, keepdims=True)
            m_ref[:, cols] = m_new
            p_ref[slot, c, :tile, :] = e.astype(BF16)
            alpha_ref[slot, :, cols] = alpha

    def values(j, slot):
        p = jnp.concatenate([p_ref[slot, c, :tile, :] for c in range(n // lanes)], axis=1)
        acc_ref[...] = alpha_ref[slot] * acc_ref[...] + _mm(vt_ref[0, 0, j], p)

    def past_shift(i):
        return (jnp.full((1, lanes), (i - t) * tile, jnp.int32).astype(F32) / inv_slope
                * sum(LOG2_E_BF16))

    def diag_bias(c):
        return diag_ref[c % (tile // lanes), :tile, :]

    def step(i, slot, bias, shift, last):
        values(jnp.maximum(i - 1, 0), 1 - slot)
        if not last:
            issue_scores(i + 1, 1 - slot)
        softmax(slot, bias, shift)

    p_ref[1] = jnp.zeros(p_ref.shape[1:], BF16)
    alpha_ref[1] = jnp.ones((1, n), F32)
    issue_scores(0, 0)

    def two_past_blocks(u, carry):
        step(2 * u, 0, None, past_shift(2 * u), False)
        step(2 * u + 1, 1, None, past_shift(2 * u + 1), False)
        return carry

    lax.fori_loop(0, t // 2, two_past_blocks, 0)
    no_shift = jnp.zeros((1, lanes), F32)

    @pl.when(t % 2 == 0)
    def _():
        step(t, 0, diag_bias, no_shift, True)
        values(t, 0)

    @pl.when(t % 2 == 1)
    def _():
        step(t - 1, 0, None, past_shift(t - 1), False)
        step(t, 1, diag_bias, no_shift, True)
        values(t, 1)

    o_t = acc_ref[...] / l_ref[...]
    lam = lam_ref[...]
    lam_full = (jnp.exp(jnp.sum(lam[0:1] * lam[1:2], axis=-1, keepdims=True))
                - jnp.exp(jnp.sum(lam[2:3] * lam[3:4], axis=-1, keepdims=True)) + lambda_init)
    a = (o_t[:, :tile] - lam_full * o_t[:, tile:]).T
    o_ref[0] = (_rms(a, subg_ref[...]) * (1.0 - lambda_init)).astype(o_ref.dtype)


def _attn_c(q_t, k, v_t, lam, subln_g, lambda_init):
    b, s, _ = k.shape
    tile = ATTN_C_TILE
    n = 2 * tile
    return pl.pallas_call(
        functools.partial(_attn_c_kernel, tile=tile, lambda_init=lambda_init),
        grid=(b, HEADS_C, s // tile),
        in_specs=[
            _resident(lam),
            _resident(subln_g),
            pl.BlockSpec((1, 1, 1, HEAD_W, tile), lambda bi, h, t: (bi, h, t, 0, 0)),
            pl.BlockSpec((1, s, HEAD_W), lambda bi, h, t: (bi, 0, h)),
            pl.BlockSpec((1, 1, s // tile, HEAD_W, tile), lambda bi, h, t: (bi, h, 0, 0, 0)),
        ],
        out_specs=pl.BlockSpec((1, tile, HEAD_W), lambda bi, h, t: (bi, t, h)),
        out_shape=jax.ShapeDtypeStruct((b, s, D_MODEL), BF16),
        scratch_shapes=[
            pltpu.VMEM((2 * HEAD_W, n), BF16),
            pltpu.VMEM((tile, HEAD_W), BF16),
            pltpu.VMEM((2, n // HEAD_W, tile + _pad_rows(F32), HEAD_W), F32),
            pltpu.VMEM((2, n // HEAD_W, tile + _pad_rows(BF16), HEAD_W), BF16),
            pltpu.VMEM((tile // HEAD_W, tile + _pad_rows(F32), HEAD_W), F32),
            pltpu.VMEM((2, 1, n), F32),
            pltpu.VMEM((1, n), F32),
            pltpu.VMEM((1, n), F32),
            pltpu.VMEM((2, 1, n), F32),
            pltpu.VMEM((HEAD_W, n), F32),
        ],
        compiler_params=_params(3),
        name="attn_c",
    )(*_operands(lam, subln_g), q_t, k, v_t)


def kernel(x, norm_g, ff1_w_in, ff1_w_out, ff2_w_in, ff2_w_out, a_w_qkv, a_rel_bias, a_w_o,
           b_w_in, b_ln_g, b_ln_b, b_w_s, b_b_s, b_w_o, c_w_qkv, c_lambda, c_subln_g, c_w_o):
    b, s, d = x.shape
    depth = norm_g.shape[0]
    ff1_w_in, ff1_w_out, ff2_w_in, ff2_w_out, a_w_o, b_w_in, b_w_s, b_w_o, c_w_o = (
        w.astype(BF16) for w in
        (ff1_w_in, ff1_w_out, ff2_w_in, ff2_w_out, a_w_o, b_w_in, b_w_s, b_w_o, c_w_o))

    def split_qkv(w):
        w = w.astype(BF16)
        return w[:, d:2 * d], jnp.concatenate([w[:, :d], w[:, 2 * d:]], axis=1).T

    b_ln_g, b_ln_b, c_subln_g = b_ln_g[:, None], b_ln_b[:, None], c_subln_g[:, None]
    b_b_s_t = jnp.swapaxes(b_b_s, 1, 2)
    xf = x.reshape(b * s, d)
    for i in range(depth):
        g = _Layer(norm_g, i)
        xf = _ffn(xf, g, _Layer(ff1_w_in, i), _Layer(ff1_w_out, i), 0)
        kind, j = i % N_MIXERS, i // N_MIXERS
        if kind == 0:
            w_k, w_qv_t = split_qkv(a_w_qkv[j])
            q_t, k, v_t = _norm_proj_t(xf, g, w_k, w_qv_t, 2, b, ATTN_A_TQ, ATTN_A_TQ)
            bias = _attn_a_bias(a_rel_bias[j], ATTN_A_TQ, ATTN_A_TQ + LEFT)
            o = _attn_a(q_t, k.reshape(b, s, d), v_t, bias)
            xf = _proj_res(o.reshape(b * s, d), _Layer(a_w_o, j), xf, g, 3)
        elif kind == 1:
            u, v = _gmlp_in(xf, g, _Layer(b_w_in, j), _Layer(b_ln_g, j), _Layer(b_ln_b, j), 2)
            xf = _gmlp_out(u, v, _Layer(b_w_s, j), _Layer(b_b_s_t, j), _Layer(b_w_o, j), xf, g, 3)
        else:
            lambda_init = 0.8 - 0.6 * math.exp(-0.3 * i)
            w_k, w_qv_t = split_qkv(c_w_qkv[j])
            q_t, k, v_t = _norm_proj_t(xf, g, w_k, w_qv_t, 2, b, ATTN_C_TILE, ATTN_C_TILE)
            o = _attn_c(q_t, k.reshape(b, s, d), v_t, _Layer(c_lambda, j), _Layer(c_subln_g, j),
                        lambda_init)
            xf = _proj_res(o.reshape(b * s, d), _Layer(c_w_o, j), xf, g, 3)
        xf = _ffn(xf, g, _Layer(ff2_w_in, i), _Layer(ff2_w_out, i), 4)
    return xf.reshape(b, s, d)
```

```python
import functools
import math
from typing import NamedTuple

import jax
import jax.numpy as jnp
import numpy as np
from jax import lax
from jax.experimental import pallas as pl
from jax.experimental.pallas import tpu as pltpu

F32 = jnp.float32
BF16 = jnp.bfloat16

D_MODEL = 1024
D_FF = 2816
EPS = 1e-6
NEG_INF = -1e30
N_MIXERS = 3

CHUNK = 64
LEFT = 8 * CHUNK
HEADS_A = 16
REL_CLIP = 128
GMLP_CHUNK = 128
GMLP_GROUPS = 8
HEADS_C = 8
HEAD_W = 128
LOG2_E = math.log2(math.e)
LOG2_E_BF16 = (1.4453125, -0.00262451171875, 7.063150405883789e-06)

V7X_VMEM_LIMIT_BYTES = 56 * 1024 * 1024
V7X_VREG_BYTES = 8 * 128 * 4


def _pad_rows(dtype):
    return V7X_VREG_BYTES // (128 * jnp.dtype(dtype).itemsize)

ROW_TILE = 512
FFN_ROW_TILE = 1024
PROJ_FFN_ROW_TILE = 1024
FFN_SUB = 256
ATTN_A_TQ = 128
ATTN_A_HEADS = 8
ATTN_C_TILE = 512


def _params(n_axes):
    return pltpu.CompilerParams(
        dimension_semantics=("arbitrary",) * n_axes,
        vmem_limit_bytes=V7X_VMEM_LIMIT_BYTES,
    )


class _Layer(NamedTuple):
    stack: jax.Array
    index: int


def _resident(p):
    if isinstance(p, _Layer):
        tail = p.stack.shape[1:]
        return pl.BlockSpec((None,) + tail, lambda *_: (p.index,) + (0,) * len(tail),
                            pipeline_mode=pl.Buffered(1))
    return pl.BlockSpec(p.shape, lambda *_: (0,) * p.ndim, pipeline_mode=pl.Buffered(1))


def _operands(*params):
    return [p.stack if isinstance(p, _Layer) else p for p in params]


def _rms(x, g):
    return x * lax.rsqrt(jnp.mean(x * x, axis=-1, keepdims=True) + EPS) * g


def _mm(a, b):
    return jnp.dot(a, b, preferred_element_type=F32)


def _mm_nt(a, b):
    return lax.dot_general(a, b, (((1,), (1,)), ((), ())), preferred_element_type=F32)


def _ffn_chains(x_ref, g, win_ref, wout_ref, o_ref, g_row):
    chains = [slice(r * FFN_SUB, (r + 1) * FFN_SUB) for r in range(x_ref.shape[0] // FFN_SUB)]

    def up_proj(rows):
        return _mm(_rms(x_ref[rows, :], g[g_row:g_row + 1]).astype(BF16), win_ref[...])

    def down_proj(gu):
        gate = gu[:, :D_FF]
        up = gu[:, D_FF:]
        h = (gate * (1.0 / (1.0 + jnp.exp(-gate))) * up).astype(BF16)
        return _mm(h, wout_ref[...])

    gu_next = up_proj(chains[0])
    y_prev = None
    for r in range(len(chains)):
        gu = gu_next
        if r + 1 < len(chains):
            gu_next = up_proj(chains[r + 1])
        y = down_proj(gu)
        if y_prev is not None:
            o_ref[chains[r - 1], :] = (x_ref[chains[r - 1], :]
                                       + 0.5 * _rms(y_prev, g[g_row + 1:g_row + 2]))
        y_prev = y
    o_ref[chains[-1], :] = x_ref[chains[-1], :] + 0.5 * _rms(y_prev, g[g_row + 1:g_row + 2])


def _ffn_kernel(x_ref, g_ref, win_ref, wout_ref, o_ref, *, g_row):
    _ffn_chains(x_ref, g_ref[...], win_ref, wout_ref, o_ref, g_row)


def _proj_ffn_kernel(a_ref, wo_ref, x_ref, g_ref, win_ref, wout_ref, o_ref, x2_ref, *, g_row):
    g = g_ref[...]
    for r in range(x_ref.shape[0] // FFN_SUB):
        rows = slice(r * FFN_SUB, (r + 1) * FFN_SUB)
        m = _mm(a_ref[rows, :], wo_ref[...])
        x2_ref[rows, :] = x_ref[rows, :] + _rms(m, g[g_row - 1:g_row])
    _ffn_chains(x2_ref, g, win_ref, wout_ref, o_ref, g_row)


def _proj_ffn(a, w_o, x, g, w_in, w_out, g_row):
    t, d = x.shape
    rows = pl.BlockSpec((PROJ_FFN_ROW_TILE, d), lambda i: (i, 0))
    return pl.pallas_call(
        functools.partial(_proj_ffn_kernel, g_row=g_row),
        grid=(t // PROJ_FFN_ROW_TILE,),
        in_specs=[rows, _resident(w_o), rows, _resident(g), _resident(w_in), _resident(w_out)],
        out_specs=rows,
        out_shape=jax.ShapeDtypeStruct((t, d), F32),
        scratch_shapes=[pltpu.VMEM((PROJ_FFN_ROW_TILE, d), F32)],
        compiler_params=_params(1),
        name="proj_ffn",
    )(a, *_operands(w_o), x, *_operands(g, w_in, w_out))


def _ffn(x, g, w_in, w_out, g_row):
    t, d = x.shape
    return pl.pallas_call(
        functools.partial(_ffn_kernel, g_row=g_row),
        grid=(t // FFN_ROW_TILE,),
        in_specs=[
            pl.BlockSpec((FFN_ROW_TILE, d), lambda i: (i, 0)),
            _resident(g),
            _resident(w_in),
            _resident(w_out),
        ],
        out_specs=pl.BlockSpec((FFN_ROW_TILE, d), lambda i: (i, 0)),
        out_shape=jax.ShapeDtypeStruct((t, d), F32),
        compiler_params=_params(1),
        name="ffn",
    )(x, *_operands(g, w_in, w_out))


N_GROUPS = D_MODEL // HEAD_W


def _norm_proj_t_kernel(x_ref, g_ref, wk_ref, wqv_ref, qt_ref, k_ref, vt_ref, *, g_row, tq, tv):
    xn = _rms(x_ref[...], g_ref[g_row:g_row + 1, :]).astype(BF16)
    k_ref[...] = _mm(xn, wk_ref[...]).astype(k_ref.dtype)
    qv_t = _mm_nt(wqv_ref[...], xn)
    rows = xn.shape[0]
    q_t = (qv_t[:D_MODEL] * ((HEAD_W // 2) ** -0.5 * LOG2_E)).astype(BF16)
    v_t = qv_t[D_MODEL:].astype(BF16)
    for part in range(rows // tq):
        qt_ref[0, :, part] = q_t[:, part * tq:(part + 1) * tq].reshape(N_GROUPS, HEAD_W, tq)
    for part in range(rows // tv):
        vt_ref[0, :, part] = v_t[:, part * tv:(part + 1) * tv].reshape(N_GROUPS, HEAD_W, tv)


def _norm_proj_t(x, g, w_k, w_qv_t, g_row, batch, tq, tv):
    t, d = x.shape
    s = t // batch
    per_batch = s // ROW_TILE

    def tiles(width):
        return pl.BlockSpec((1, N_GROUPS, ROW_TILE // width, HEAD_W, width),
                            lambda i: (i // per_batch, 0, i % per_batch, 0, 0))

    return pl.pallas_call(
        functools.partial(_norm_proj_t_kernel, g_row=g_row, tq=tq, tv=tv),
        grid=(t // ROW_TILE,),
        in_specs=[
            pl.BlockSpec((ROW_TILE, d), lambda i: (i, 0)),
            _resident(g),
            _resident(w_k),
            _resident(w_qv_t),
        ],
        out_specs=[tiles(tq), pl.BlockSpec((ROW_TILE, d), lambda i: (i, 0)), tiles(tv)],
        out_shape=[
            jax.ShapeDtypeStruct((batch, N_GROUPS, s // tq, HEAD_W, tq), BF16),
            jax.ShapeDtypeStruct((t, d), BF16),
            jax.ShapeDtypeStruct((batch, N_GROUPS, s // tv, HEAD_W, tv), BF16),
        ],
        compiler_params=_params(1),
        name="norm_proj_t",
    )(x, *_operands(g, w_k, w_qv_t))


def _attn_a_bias(rel_bias, tq, win):
    heads = rel_bias.shape[0]
    tbl = rel_bias.astype(F32)
    f = jnp.concatenate([
        jnp.broadcast_to(tbl[:, 2 * REL_CLIP:], (heads, tq + LEFT - REL_CLIP)),
        tbl[:, 2 * REL_CLIP - 1:0:-1],
    ], axis=1)
    p = win + tq
    skew = jnp.tile(jnp.pad(f, ((0, 0), (0, 1))), (1, tq))[:, :tq * (p - 1)]
    bias = skew.reshape(heads, tq, p - 1)[:, :, tq - 1:tq - 1 + win]
    r = np.arange(tq)[:, None] // CHUNK
    j = np.arange(win)[None, :]
    valid = (j >= r * CHUNK) & (j < (r + LEFT // CHUNK + 1) * CHUNK)
    bias = jnp.where(valid, bias * LOG2_E, NEG_INF)
    bias = jnp.pad(bias, ((0, 0), (0, 0), (0, LEFT)), constant_values=NEG_INF)
    bias = bias.reshape(HEADS_A // 2, 2, tq, win + LEFT)
    return jnp.transpose(bias, (0, 3, 1, 2)).reshape(HEADS_A // 2, win + LEFT, 2 * tq)


def _attn_a_kernel(qt_ref, k_ref, vt_ref, bias_ref, o_ref, *, tq, win):
    t = pl.program_id(2)
    wb = jnp.maximum(t - LEFT // tq, 0)
    ws = pl.multiple_of(wb * tq, tq)
    shift = pl.multiple_of(jnp.maximum(LEFT - t * tq, 0), tq)
    first = lax.broadcasted_iota(jnp.int32, (HEAD_W, tq), 0) < HEAD_W // 2
    pairs = ATTN_A_HEADS // 2

    def scores(pp):
        zero = jnp.zeros((HEAD_W, tq), BF16)
        blocks = []
        for p in (2 * pp, 2 * pp + 1):
            qt = qt_ref[0, p, 0]
            row = [zero] * 4
            row[2 * (p % 2)] = jnp.where(first, qt, zero)
            row[2 * (p % 2) + 1] = jnp.where(first, zero, qt)
            blocks.append(jnp.concatenate(row, axis=1))
        qs = jnp.concatenate(blocks, axis=0)
        keys = k_ref[0, pl.ds(ws, win), 2 * pp * HEAD_W:(2 * pp + 2) * HEAD_W]
        bias = jnp.concatenate([bias_ref[2 * pp, pl.ds(shift, win), :],
                                bias_ref[2 * pp + 1, pl.ds(shift, win), :]], axis=1)
        return _mm(keys, qs) + bias

    def attend(p, s):
        m = jnp.max(s, axis=0, keepdims=True)
        e = jnp.exp2(s - m)
        l = jnp.sum(e, axis=0, keepdims=True)
        v_t = jnp.concatenate([vt_ref[0, p, wb + c] for c in range(win // tq)], axis=1)
        o_t = _mm(v_t, e.astype(BF16)) * (1.0 / l)
        o_t = jnp.where(first, o_t[:, :tq], o_t[:, tq:])
        o_ref[0, :, p * HEAD_W:(p + 1) * HEAD_W] = o_t.T.astype(o_ref.dtype)

    s_next = scores(0)
    for pp in range(pairs // 2):
        s_cur = s_next
        if pp + 1 < pairs // 2:
            s_next = scores(pp + 1)
        attend(2 * pp, s_cur[:, :2 * tq])
        attend(2 * pp + 1, s_cur[:, 2 * tq:])


def _attn_a(q_t, k, v_t, bias):
    b, s, _ = k.shape
    tq = ATTN_A_TQ
    win = tq + LEFT
    pairs = ATTN_A_HEADS // 2
    gw = pairs * HEAD_W
    assert bias.shape == (HEADS_A // 2, win + LEFT, 2 * tq)
    return pl.pallas_call(
        functools.partial(_attn_a_kernel, tq=tq, win=win),
        grid=(b, D_MODEL // gw, s // tq),
        in_specs=[
            pl.BlockSpec((1, pairs, 1, HEAD_W, tq), lambda bi, hg, t: (bi, hg, t, 0, 0)),
            pl.BlockSpec((1, s, gw), lambda bi, hg, t: (bi, 0, hg)),
            pl.BlockSpec((1, pairs, s // tq, HEAD_W, tq), lambda bi, hg, t: (bi, hg, 0, 0, 0)),
            pl.BlockSpec((pairs, win + LEFT, 2 * tq), lambda bi, hg, t: (hg, 0, 0)),
        ],
        out_specs=pl.BlockSpec((1, tq, gw), lambda bi, hg, t: (bi, t, hg)),
        out_shape=jax.ShapeDtypeStruct((b, s, D_MODEL), BF16),
        compiler_params=_params(3),
        name="attn_a",
    )(q_t, k, v_t, bias)


def _gmlp_in_kernel(x_ref, g_ref, w_ref, lng_ref, lnb_ref, u_ref, v_ref, *, g_row):
    chains = [slice(r * FFN_SUB, (r + 1) * FFN_SUB) for r in range(x_ref.shape[0] // FFN_SUB)]
    hws = [_mm(_rms(x_ref[rows, :], g_ref[g_row:g_row + 1, :]).astype(BF16), w_ref[...])
           for rows in chains]
    for rows, hw in zip(chains, hws):
        act = 0.5 * hw * (1.0 + lax.erf(hw * math.sqrt(0.5)))
        u_ref[rows, :] = act[:, :D_MODEL].astype(u_ref.dtype)
        v = act[:, D_MODEL:]
        mu = jnp.mean(v, axis=-1, keepdims=True)
        vc = v - mu
        var = jnp.mean(vc * vc, axis=-1, keepdims=True)
        v_ref[rows, :] = (vc * lax.rsqrt(var + EPS) * lng_ref[...] + lnb_ref[...]).astype(v_ref.dtype)


def _gmlp_in(x, g, w, ln_g, ln_b, g_row):
    t, d = x.shape
    out = jax.ShapeDtypeStruct((t, d), BF16)
    row_spec = pl.BlockSpec((ROW_TILE, d), lambda i: (i, 0))
    return pl.pallas_call(
        functools.partial(_gmlp_in_kernel, g_row=g_row),
        grid=(t // ROW_TILE,),
        in_specs=[row_spec, _resident(g), _resident(w),
                  _resident(ln_g), _resident(ln_b)],
        out_specs=[row_spec, row_spec],
        out_shape=[out, out],
        compiler_params=_params(1),
        name="gmlp_in",
    )(x, *_operands(g, w, ln_g, ln_b))


def _gmlp_out_kernel(u_ref, v_ref, ws_ref, bs_ref, wo_ref, x_ref, g_ref, out_ref, y_ref, *, g_row):
    c = GMLP_CHUNK
    row = lax.broadcasted_iota(jnp.int32, (c, c), 0)
    col = lax.broadcasted_iota(jnp.int32, (c, c), 1)
    causal = row >= col
    for grp in range(GMLP_GROUPS):
        w = ws_ref[grp]
        w = jnp.where(causal, w, jnp.zeros_like(w))
        b = bs_ref[:, grp:grp + 1]
        lanes = slice(grp * c, (grp + 1) * c)
        for blk in range(ROW_TILE // c):
            rows = slice(blk * c, (blk + 1) * c)
            sv = _mm(w, v_ref[rows, lanes]) + b
            y_ref[rows, lanes] = (u_ref[rows, lanes].astype(F32) * sv).astype(y_ref.dtype)
    m = _mm(y_ref[...], wo_ref[...])
    out_ref[...] = x_ref[...] + _rms(m, g_ref[g_row:g_row + 1, :])


def _gmlp_out(u, v, w_s, b_s_t, w_o, x, g, g_row):
    t, d = x.shape
    row_spec = pl.BlockSpec((ROW_TILE, d), lambda i: (i, 0))
    return pl.pallas_call(
        functools.partial(_gmlp_out_kernel, g_row=g_row),
        grid=(t // ROW_TILE,),
        in_specs=[row_spec, row_spec, _resident(w_s), _resident(b_s_t),
                  _resident(w_o), row_spec, _resident(g)],
        out_specs=row_spec,
        out_shape=jax.ShapeDtypeStruct((t, d), F32),
        scratch_shapes=[pltpu.VMEM((ROW_TILE, d), BF16)],
        compiler_params=_params(1),
        name="gmlp_out",
    )(u, v, *_operands(w_s, b_s_t, w_o), x, *_operands(g))


def _attn_c_kernel(lam_ref, subg_ref, qt_ref, k_ref, vt_ref, o_ref,
                   qs_ref, kaug_ref, s_ref, p_ref, diag_ref, smax_ref, m_ref, l_ref, alpha_ref,
                   acc_ref,
                   *, tile, lambda_init):
    h = pl.program_id(1)
    t = pl.program_id(2)
    n = 2 * tile
    lanes = HEAD_W
    group = tile
    n_aug = 2 * len(LOG2_E_BF16)
    inv_slope = jnp.left_shift(1, jnp.full((1, lanes), h + 1, jnp.int32)).astype(F32)

    @pl.when(t == 0)
    def _():
        key = lax.broadcasted_iota(jnp.int32, (tile, lanes), 0)
        lane = lax.broadcasted_iota(jnp.int32, (tile, lanes), 1)
        lo = (key & 255).astype(F32) / inv_slope
        hi = (key & 256).astype(F32) / inv_slope
        kaug_ref[...] = jnp.where(lane < n_aug // 2, lo,
                                  jnp.where(lane < n_aug, hi, 0.0)).astype(BF16)
        row = lax.broadcasted_iota(jnp.int32, (HEAD_W, n), 0)
        consts = jnp.zeros((HEAD_W, n), F32)
        for i, c in enumerate(LOG2_E_BF16 + LOG2_E_BF16):
            consts = jnp.where(row == i, c, consts)
        qs_ref[HEAD_W:, :] = consts.astype(BF16)
        for c in range(tile // lanes):
            qry = lax.broadcasted_iota(jnp.int32, (tile, lanes), 1) + c * lanes
            future = jnp.minimum(2 * (qry - key), 0).astype(F32) / inv_slope * LOG2_E
            allowed = jnp.right_shift(key, 6) <= jnp.right_shift(qry, 6)
            diag_ref[c, :tile, :] = jnp.where(allowed, future, NEG_INF)

    qt = qt_ref[0, 0, 0]
    first = lax.broadcasted_iota(jnp.int32, (HEAD_W, tile), 0) < HEAD_W // 2
    zero = jnp.zeros_like(qt)
    qs_ref[:HEAD_W, :tile] = jnp.where(first, qt, zero)
    qs_ref[:HEAD_W, tile:] = jnp.where(first, zero, qt)

    m_ref[...] = jnp.full(m_ref.shape, NEG_INF, F32)
    l_ref[...] = jnp.zeros(l_ref.shape, F32)
    acc_ref[...] = jnp.zeros(acc_ref.shape, F32)

    def issue_scores(j, slot):
        start = pl.multiple_of(j * tile, tile)
        keys = jnp.concatenate([k_ref[0, pl.ds(start, tile), :], kaug_ref[...]], axis=1)
        for g in range(n // group):
            gcols = slice(g * group, (g + 1) * group)
            s = _mm(keys, qs_ref[:, gcols])
            for c in range(group // lanes):
                s_ref[slot, g * (group // lanes) + c, :tile, :] = s[:, c * lanes:(c + 1) * lanes]
            smax_ref[slot, :, gcols] = jnp.max(s, axis=0, keepdims=True)

    def softmax(slot, bias, shift):
        for c in range(n // lanes):
            cols = slice(c * lanes, (c + 1) * lanes)
            if bias is None:
                smax = smax_ref[slot, :, cols]
            else:
                sc = s_ref[slot, c, :tile, :] + bias(c)
                s_ref[slot, c, :tile, :] = sc
                smax = jnp.max(sc, axis=0, keepdims=True)
            m_old = m_ref[:, cols]
            m_new = jnp.maximum(m_old, smax + shift)
            alpha = jnp.exp2(m_old - m_new)
            e = jnp.exp2(s_ref[slot, c, :tile, :] - (m_new - shift))
            l_ref[:, cols] = alpha * l_ref[:, cols] + jnp.sum(e, axis=0, keepdims=True)
            m_ref[:, cols] = m_new
            p_ref[slot, c, :tile, :] = e.astype(BF16)
            alpha_ref[slot, :, cols] = alpha

    def values(j, slot):
        p = jnp.concatenate([p_ref[slot, c, :tile, :] for c in range(n // lanes)], axis=1)
        acc_ref[...] = alpha_ref[slot] * acc_ref[...] + _mm(vt_ref[0, 0, j], p)

    def past_shift(i):
        return (jnp.full((1, lanes), (i - t) * tile, jnp.int32).astype(F32) / inv_slope
                * sum(LOG2_E_BF16))

    def diag_bias(c):
        return diag_ref[c % (tile // lanes), :tile, :]

    def step(i, slot, bias, shift, last):
        values(jnp.maximum(i - 1, 0), 1 - slot)
        if not last:
            issue_scores(i + 1, 1 - slot)
        softmax(slot, bias, shift)

    p_ref[1] = jnp.zeros(p_ref.shape[1:], BF16)
    alpha_ref[1] = jnp.ones((1, n), F32)
    issue_scores(0, 0)

    def two_past_blocks(u, carry):
        step(2 * u, 0, None, past_shift(2 * u), False)
        step(2 * u + 1, 1, None, past_shift(2 * u + 1), False)
        return carry

    lax.fori_loop(0, t // 2, two_past_blocks, 0)
    no_shift = jnp.zeros((1, lanes), F32)

    @pl.when(t % 2 == 0)
    def _():
        step(t, 0, diag_bias, no_shift, True)
        values(t, 0)

    @pl.when(t % 2 == 1)
    def _():
        step(t - 1, 0, None, past_shift(t - 1), False)
        step(t, 1, diag_bias, no_shift, True)
        values(t, 1)

    o_t = acc_ref[...] * (1.0 / l_ref[...])
    lam = lam_ref[...]
    lam_full = (jnp.exp(jnp.sum(lam[0:1] * lam[1:2], axis=-1, keepdims=True))
                - jnp.exp(jnp.sum(lam[2:3] * lam[3:4], axis=-1, keepdims=True)) + lambda_init)
    a = (o_t[:, :tile] - lam_full * o_t[:, tile:]).T
    o_ref[0] = (_rms(a, subg_ref[...]) * (1.0 - lambda_init)).astype(o_ref.dtype)


def _attn_c(q_t, k, v_t, lam, subln_g, lambda_init):
    b, s, _ = k.shape
    tile = ATTN_C_TILE
    n = 2 * tile
    return pl.pallas_call(
        functools.partial(_attn_c_kernel, tile=tile, lambda_init=lambda_init),
        grid=(b, HEADS_C, s // tile),
        in_specs=[
            _resident(lam),
            _resident(subln_g),
            pl.BlockSpec((1, 1, 1, HEAD_W, tile), lambda bi, h, t: (bi, h, t, 0, 0)),
            pl.BlockSpec((1, s, HEAD_W), lambda bi, h, t: (bi, 0, h)),
            pl.BlockSpec((1, 1, s // tile, HEAD_W, tile), lambda bi, h, t: (bi, h, 0, 0, 0)),
        ],
        out_specs=pl.BlockSpec((1, tile, HEAD_W), lambda bi, h, t: (bi, t, h)),
        out_shape=jax.ShapeDtypeStruct((b, s, D_MODEL), BF16),
        scratch_shapes=[
            pltpu.VMEM((2 * HEAD_W, n), BF16),
            pltpu.VMEM((tile, HEAD_W), BF16),
            pltpu.VMEM((2, n // HEAD_W, tile + _pad_rows(F32), HEAD_W), F32),
            pltpu.VMEM((2, n // HEAD_W, tile + _pad_rows(BF16), HEAD_W), BF16),
            pltpu.VMEM((tile // HEAD_W, tile + _pad_rows(F32), HEAD_W), F32),
            pltpu.VMEM((2, 1, n), F32),
            pltpu.VMEM((1, n), F32),
            pltpu.VMEM((1, n), F32),
            pltpu.VMEM((2, 1, n), F32),
            pltpu.VMEM((HEAD_W, n), F32),
        ],
        compiler_params=_params(3),
        name="attn_c",
    )(*_operands(lam, subln_g), q_t, k, v_t)


def kernel(x, norm_g, ff1_w_in, ff1_w_out, ff2_w_in, ff2_w_out, a_w_qkv, a_rel_bias, a_w_o,
           b_w_in, b_ln_g, b_ln_b, b_w_s, b_b_s, b_w_o, c_w_qkv, c_lambda, c_subln_g, c_w_o):
    b, s, d = x.shape
    depth = norm_g.shape[0]
    ff1_w_in, ff1_w_out, ff2_w_in, ff2_w_out, a_w_o, b_w_in, b_w_s, b_w_o, c_w_o = (
        w.astype(BF16) for w in
        (ff1_w_in, ff1_w_out, ff2_w_in, ff2_w_out, a_w_o, b_w_in, b_w_s, b_w_o, c_w_o))

    def split_qkv(w):
        w = w.astype(BF16)
        return w[:, d:2 * d], jnp.concatenate([w[:, :d], w[:, 2 * d:]], axis=1).T

    b_ln_g, b_ln_b, c_subln_g = b_ln_g[:, None], b_ln_b[:, None], c_subln_g[:, None]
    b_b_s_t = jnp.swapaxes(b_b_s, 1, 2)
    xf = x.reshape(b * s, d)
    for i in range(depth):
        g = _Layer(norm_g, i)
        xf = _ffn(xf, g, _Layer(ff1_w_in, i), _Layer(ff1_w_out, i), 0)
        ff2 = (_Layer(ff2_w_in, i), _Layer(ff2_w_out, i), 4)
        kind, j = i % N_MIXERS, i // N_MIXERS
        if kind == 0:
            w_k, w_qv_t = split_qkv(a_w_qkv[j])
            q_t, k, v_t = _norm_proj_t(xf, g, w_k, w_qv_t, 2, b, ATTN_A_TQ, ATTN_A_TQ)
            bias = _attn_a_bias(a_rel_bias[j], ATTN_A_TQ, ATTN_A_TQ + LEFT)
            o = _attn_a(q_t, k.reshape(b, s, d), v_t, bias)
            xf = _proj_ffn(o.reshape(b * s, d), _Layer(a_w_o, j), xf, g, *ff2)
        elif kind == 1:
            u, v = _gmlp_in(xf, g, _Layer(b_w_in, j), _Layer(b_ln_g, j), _Layer(b_ln_b, j), 2)
            xf = _gmlp_out(u, v, _Layer(b_w_s, j), _Layer(b_b_s_t, j), _Layer(b_w_o, j), xf, g, 3)
            xf = _ffn(xf, g, *ff2)
        else:
            lambda_init = 0.8 - 0.6 * math.exp(-0.3 * i)
            w_k, w_qv_t = split_qkv(c_w_qkv[j])
            q_t, k, v_t = _norm_proj_t(xf, g, w_k, w_qv_t, 2, b, ATTN_C_TILE, ATTN_C_TILE)
            o = _attn_c(q_t, k.reshape(b, s, d), v_t, _Layer(c_lambda, j), _Layer(c_subln_g, j),
                        lambda_init)
            xf = _proj_ffn(o.reshape(b * s, d), _Layer(c_w_o, j), xf, g, *ff2)
    return xf.reshape(b, s, d)
```

```python
import functools
import math
from typing import NamedTuple

import jax
import jax.numpy as jnp
import numpy as np
from jax import lax
from jax.experimental import pallas as pl
from jax.experimental.pallas import tpu as pltpu

F32 = jnp.float32
BF16 = jnp.bfloat16

D_MODEL = 1024
D_FF = 2816
EPS = 1e-6
NEG_INF = -1e30
N_MIXERS = 3

CHUNK = 64
LEFT = 8 * CHUNK
HEADS_A = 16
REL_CLIP = 128
GMLP_CHUNK = 128
GMLP_GROUPS = 8
HEADS_C = 8
HEAD_W = 128
LOG2_E = math.log2(math.e)
LOG2_E_BF16 = (1.4453125, -0.00262451171875, 7.063150405883789e-06)

V7X_VMEM_LIMIT_BYTES = 56 * 1024 * 1024
V7X_VREG_BYTES = 8 * 128 * 4


def _pad_rows(dtype):
    return V7X_VREG_BYTES // (128 * jnp.dtype(dtype).itemsize)

ROW_TILE = 512
FFN_ROW_TILE = 1024
PROJ_FFN_ROW_TILE = 1024
FFN_SUB = 256
ATTN_A_TQ = 128
ATTN_A_HEADS = 16
ATTN_C_TILE = 512
ATTN_C_UNROLL = 2


def _params(n_axes):
    return pltpu.CompilerParams(
        dimension_semantics=("arbitrary",) * n_axes,
        vmem_limit_bytes=V7X_VMEM_LIMIT_BYTES,
    )


class _Layer(NamedTuple):
    stack: jax.Array
    index: int


def _resident(p):
    if isinstance(p, _Layer):
        tail = p.stack.shape[1:]
        return pl.BlockSpec((None,) + tail, lambda *_: (p.index,) + (0,) * len(tail),
                            pipeline_mode=pl.Buffered(1))
    return pl.BlockSpec(p.shape, lambda *_: (0,) * p.ndim, pipeline_mode=pl.Buffered(1))


def _operands(*params):
    return [p.stack if isinstance(p, _Layer) else p for p in params]


def _rms(x, g):
    return x * lax.rsqrt(jnp.mean(x * x, axis=-1, keepdims=True) + EPS) * g


def _mm(a, b):
    return jnp.dot(a, b, preferred_element_type=F32)


def _mm_nt(a, b):
    return lax.dot_general(a, b, (((1,), (1,)), ((), ())), preferred_element_type=F32)


def _ffn_chains(x_ref, g, win_ref, wout_ref, o_ref, g_row):
    chains = [slice(r * FFN_SUB, (r + 1) * FFN_SUB) for r in range(x_ref.shape[0] // FFN_SUB)]

    def up_proj(rows):
        return _mm(_rms(x_ref[rows, :], g[g_row:g_row + 1]).astype(BF16), win_ref[...])

    def down_proj(gu):
        gate = gu[:, :D_FF]
        up = gu[:, D_FF:]
        h = (gate * (1.0 / (1.0 + jnp.exp(-gate))) * up).astype(BF16)
        return _mm(h, wout_ref[...])

    gu_next = up_proj(chains[0])
    y_prev = None
    for r in range(len(chains)):
        gu = gu_next
        if r + 1 < len(chains):
            gu_next = up_proj(chains[r + 1])
        y = down_proj(gu)
        if y_prev is not None:
            o_ref[chains[r - 1], :] = (x_ref[chains[r - 1], :]
                                       + 0.5 * _rms(y_prev, g[g_row + 1:g_row + 2]))
        y_prev = y
    o_ref[chains[-1], :] = x_ref[chains[-1], :] + 0.5 * _rms(y_prev, g[g_row + 1:g_row + 2])


def _ffn_kernel(x_ref, g_ref, win_ref, wout_ref, o_ref, *, g_row):
    _ffn_chains(x_ref, g_ref[...], win_ref, wout_ref, o_ref, g_row)


def _proj_ffn_kernel(a_ref, wo_ref, x_ref, g_ref, win_ref, wout_ref, o_ref, x2_ref, *, g_row):
    g = g_ref[...]
    for r in range(x_ref.shape[0] // FFN_SUB):
        rows = slice(r * FFN_SUB, (r + 1) * FFN_SUB)
        m = _mm(a_ref[rows, :], wo_ref[...])
        x2_ref[rows, :] = x_ref[rows, :] + _rms(m, g[g_row - 1:g_row])
    _ffn_chains(x2_ref, g, win_ref, wout_ref, o_ref, g_row)


def _proj_ffn(a, w_o, x, g, w_in, w_out, g_row):
    t, d = x.shape
    rows = pl.BlockSpec((PROJ_FFN_ROW_TILE, d), lambda i: (i, 0))
    return pl.pallas_call(
        functools.partial(_proj_ffn_kernel, g_row=g_row),
        grid=(t // PROJ_FFN_ROW_TILE,),
        in_specs=[rows, _resident(w_o), rows, _resident(g), _resident(w_in), _resident(w_out)],
        out_specs=rows,
        out_shape=jax.ShapeDtypeStruct((t, d), F32),
        scratch_shapes=[pltpu.VMEM((PROJ_FFN_ROW_TILE, d), F32)],
        compiler_params=_params(1),
        name="proj_ffn",
    )(a, *_operands(w_o), x, *_operands(g, w_in, w_out))


def _ffn(x, g, w_in, w_out, g_row):
    t, d = x.shape
    return pl.pallas_call(
        functools.partial(_ffn_kernel, g_row=g_row),
        grid=(t // FFN_ROW_TILE,),
        in_specs=[
            pl.BlockSpec((FFN_ROW_TILE, d), lambda i: (i, 0)),
            _resident(g),
            _resident(w_in),
            _resident(w_out),
        ],
        out_specs=pl.BlockSpec((FFN_ROW_TILE, d), lambda i: (i, 0)),
        out_shape=jax.ShapeDtypeStruct((t, d), F32),
        compiler_params=_params(1),
        name="ffn",
    )(x, *_operands(g, w_in, w_out))


N_GROUPS = D_MODEL // HEAD_W


def _norm_proj_t_kernel(x_ref, g_ref, wk_ref, wqv_ref, qt_ref, k_ref, vt_ref, *, g_row, tq, tv):
    xn = _rms(x_ref[...], g_ref[g_row:g_row + 1, :]).astype(BF16)
    k_ref[...] = _mm(xn, wk_ref[...]).astype(k_ref.dtype)
    qv_t = _mm_nt(wqv_ref[...], xn)
    rows = xn.shape[0]
    q_t = (qv_t[:D_MODEL] * ((HEAD_W // 2) ** -0.5 * LOG2_E)).astype(BF16)
    v_t = qv_t[D_MODEL:].astype(BF16)
    for part in range(rows // tq):
        qt_ref[0, :, part] = q_t[:, part * tq:(part + 1) * tq].reshape(N_GROUPS, HEAD_W, tq)
    for part in range(rows // tv):
        vt_ref[0, :, part] = v_t[:, part * tv:(part + 1) * tv].reshape(N_GROUPS, HEAD_W, tv)


def _norm_proj_t(x, g, w_k, w_qv_t, g_row, batch, tq, tv):
    t, d = x.shape
    s = t // batch
    per_batch = s // ROW_TILE

    def tiles(width):
        return pl.BlockSpec((1, N_GROUPS, ROW_TILE // width, HEAD_W, width),
                            lambda i: (i // per_batch, 0, i % per_batch, 0, 0))

    return pl.pallas_call(
        functools.partial(_norm_proj_t_kernel, g_row=g_row, tq=tq, tv=tv),
        grid=(t // ROW_TILE,),
        in_specs=[
            pl.BlockSpec((ROW_TILE, d), lambda i: (i, 0)),
            _resident(g),
            _resident(w_k),
            _resident(w_qv_t),
        ],
        out_specs=[tiles(tq), pl.BlockSpec((ROW_TILE, d), lambda i: (i, 0)), tiles(tv)],
        out_shape=[
            jax.ShapeDtypeStruct((batch, N_GROUPS, s // tq, HEAD_W, tq), BF16),
            jax.ShapeDtypeStruct((t, d), BF16),
            jax.ShapeDtypeStruct((batch, N_GROUPS, s // tv, HEAD_W, tv), BF16),
        ],
        compiler_params=_params(1),
        name="norm_proj_t",
    )(x, *_operands(g, w_k, w_qv_t))


def _attn_a_bias(rel_bias, tq, win):
    heads = rel_bias.shape[0]
    tbl = rel_bias.astype(F32)
    f = jnp.concatenate([
        jnp.broadcast_to(tbl[:, 2 * REL_CLIP:], (heads, tq + LEFT - REL_CLIP)),
        tbl[:, 2 * REL_CLIP - 1:0:-1],
    ], axis=1)
    p = win + tq
    skew = jnp.tile(jnp.pad(f, ((0, 0), (0, 1))), (1, tq))[:, :tq * (p - 1)]
    bias = skew.reshape(heads, tq, p - 1)[:, :, tq - 1:tq - 1 + win]
    r = np.arange(tq)[:, None] // CHUNK
    j = np.arange(win)[None, :]
    valid = (j >= r * CHUNK) & (j < (r + LEFT // CHUNK + 1) * CHUNK)
    bias = jnp.where(valid, bias * LOG2_E, NEG_INF)
    bias = jnp.pad(bias, ((0, 0), (0, 0), (0, LEFT)), constant_values=NEG_INF)
    bias = bias.reshape(HEADS_A // 2, 2, tq, win + LEFT)
    return jnp.transpose(bias, (0, 3, 1, 2)).reshape(HEADS_A // 2, win + LEFT, 2 * tq)


def _attn_a_kernel(qt_ref, k_ref, vt_ref, bias_ref, o_ref, *, tq, win):
    t = pl.program_id(2)
    wb = jnp.maximum(t - LEFT // tq, 0)
    ws = pl.multiple_of(wb * tq, tq)
    shift = pl.multiple_of(jnp.maximum(LEFT - t * tq, 0), tq)
    first = lax.broadcasted_iota(jnp.int32, (HEAD_W, tq), 0) < HEAD_W // 2
    pairs = ATTN_A_HEADS // 2

    def scores(pp):
        zero = jnp.zeros((HEAD_W, tq), BF16)
        blocks = []
        for p in (2 * pp, 2 * pp + 1):
            qt = qt_ref[0, p, 0]
            row = [zero] * 4
            row[2 * (p % 2)] = jnp.where(first, qt, zero)
            row[2 * (p % 2) + 1] = jnp.where(first, zero, qt)
            blocks.append(jnp.concatenate(row, axis=1))
        qs = jnp.concatenate(blocks, axis=0)
        keys = k_ref[0, pl.ds(ws, win), 2 * pp * HEAD_W:(2 * pp + 2) * HEAD_W]
        bias = jnp.concatenate([bias_ref[2 * pp, pl.ds(shift, win), :],
                                bias_ref[2 * pp + 1, pl.ds(shift, win), :]], axis=1)
        return _mm(keys, qs) + bias

    def attend(p, s):
        m = jnp.max(s, axis=0, keepdims=True)
        e = jnp.exp2(s - m)
        l = jnp.sum(e, axis=0, keepdims=True)
        v_t = jnp.concatenate([vt_ref[0, p, wb + c] for c in range(win // tq)], axis=1)
        o_t = _mm(v_t, e.astype(BF16)) * (1.0 / l)
        o_t = jnp.where(first, o_t[:, :tq], o_t[:, tq:])
        o_ref[0, :, p * HEAD_W:(p + 1) * HEAD_W] = o_t.T.astype(o_ref.dtype)

    s_next = scores(0)
    for pp in range(pairs // 2):
        s_cur = s_next
        if pp + 1 < pairs // 2:
            s_next = scores(pp + 1)
        attend(2 * pp, s_cur[:, :2 * tq])
        attend(2 * pp + 1, s_cur[:, 2 * tq:])


def _attn_a(q_t, k, v_t, bias):
    b, s, _ = k.shape
    tq = ATTN_A_TQ
    win = tq + LEFT
    pairs = ATTN_A_HEADS // 2
    gw = pairs * HEAD_W
    assert bias.shape == (HEADS_A // 2, win + LEFT, 2 * tq)
    return pl.pallas_call(
        functools.partial(_attn_a_kernel, tq=tq, win=win),
        grid=(b, D_MODEL // gw, s // tq),
        in_specs=[
            pl.BlockSpec((1, pairs, 1, HEAD_W, tq), lambda bi, hg, t: (bi, hg, t, 0, 0)),
            pl.BlockSpec((1, s, gw), lambda bi, hg, t: (bi, 0, hg), pipeline_mode=pl.Buffered(1)),
            pl.BlockSpec((1, pairs, s // tq, HEAD_W, tq), lambda bi, hg, t: (bi, hg, 0, 0, 0),
                         pipeline_mode=pl.Buffered(1)),
            pl.BlockSpec((pairs, win + LEFT, 2 * tq), lambda bi, hg, t: (hg, 0, 0),
                         pipeline_mode=pl.Buffered(1)),
        ],
        out_specs=pl.BlockSpec((1, tq, gw), lambda bi, hg, t: (bi, t, hg)),
        out_shape=jax.ShapeDtypeStruct((b, s, D_MODEL), BF16),
        compiler_params=_params(3),
        name="attn_a",
    )(q_t, k, v_t, bias)


def _gmlp_in_kernel(x_ref, g_ref, w_ref, lng_ref, lnb_ref, u_ref, v_ref, *, g_row):
    chains = [slice(r * FFN_SUB, (r + 1) * FFN_SUB) for r in range(x_ref.shape[0] // FFN_SUB)]
    hws = [_mm(_rms(x_ref[rows, :], g_ref[g_row:g_row + 1, :]).astype(BF16), w_ref[...])
           for rows in chains]
    for rows, hw in zip(chains, hws):
        act = 0.5 * hw * (1.0 + lax.erf(hw * math.sqrt(0.5)))
        u_ref[rows, :] = act[:, :D_MODEL].astype(u_ref.dtype)
        v = act[:, D_MODEL:]
        mu = jnp.mean(v, axis=-1, keepdims=True)
        vc = v - mu
        var = jnp.mean(vc * vc, axis=-1, keepdims=True)
        v_ref[rows, :] = (vc * lax.rsqrt(var + EPS) * lng_ref[...] + lnb_ref[...]).astype(v_ref.dtype)


def _gmlp_in(x, g, w, ln_g, ln_b, g_row):
    t, d = x.shape
    out = jax.ShapeDtypeStruct((t, d), BF16)
    row_spec = pl.BlockSpec((ROW_TILE, d), lambda i: (i, 0))
    return pl.pallas_call(
        functools.partial(_gmlp_in_kernel, g_row=g_row),
        grid=(t // ROW_TILE,),
        in_specs=[row_spec, _resident(g), _resident(w),
                  _resident(ln_g), _resident(ln_b)],
        out_specs=[row_spec, row_spec],
        out_shape=[out, out],
        compiler_params=_params(1),
        name="gmlp_in",
    )(x, *_operands(g, w, ln_g, ln_b))


def _gmlp_ffn_kernel(u_ref, v_ref, ws_ref, bs_ref, wo_ref, x_ref, g_ref, win_ref, wout_ref,
                     out_ref, y_ref, x2_ref, *, g_row):
    c = GMLP_CHUNK
    row = lax.broadcasted_iota(jnp.int32, (c, c), 0)
    col = lax.broadcasted_iota(jnp.int32, (c, c), 1)
    causal = row >= col
    for grp in range(GMLP_GROUPS):
        w = ws_ref[grp]
        w = jnp.where(causal, w, jnp.zeros_like(w))
        b = bs_ref[:, grp:grp + 1]
        lanes = slice(grp * c, (grp + 1) * c)
        for blk in range(x_ref.shape[0] // c):
            rows = slice(blk * c, (blk + 1) * c)
            sv = _mm(w, v_ref[rows, lanes]) + b
            y_ref[rows, lanes] = (u_ref[rows, lanes].astype(F32) * sv).astype(y_ref.dtype)
    g = g_ref[...]
    for r in range(x_ref.shape[0] // FFN_SUB):
        rows = slice(r * FFN_SUB, (r + 1) * FFN_SUB)
        m = _mm(y_ref[rows, :], wo_ref[...])
        x2_ref[rows, :] = x_ref[rows, :] + _rms(m, g[g_row - 1:g_row])
    _ffn_chains(x2_ref, g, win_ref, wout_ref, out_ref, g_row)


def _gmlp_ffn(u, v, w_s, b_s_t, w_o, x, g, w_in, w_out, g_row):
    t, d = x.shape
    row_spec = pl.BlockSpec((ROW_TILE, d), lambda i: (i, 0))
    return pl.pallas_call(
        functools.partial(_gmlp_ffn_kernel, g_row=g_row),
        grid=(t // ROW_TILE,),
        in_specs=[row_spec, row_spec, _resident(w_s), _resident(b_s_t), _resident(w_o), row_spec,
                  _resident(g), _resident(w_in), _resident(w_out)],
        out_specs=row_spec,
        out_shape=jax.ShapeDtypeStruct((t, d), F32),
        scratch_shapes=[pltpu.VMEM((ROW_TILE, d), BF16), pltpu.VMEM((ROW_TILE, d), F32)],
        compiler_params=_params(1),
        name="gmlp_ffn",
    )(u, v, *_operands(w_s, b_s_t, w_o), x, *_operands(g, w_in, w_out))


def _attn_c_kernel(lam_ref, subg_ref, qt_ref, k_ref, vt_ref, o_ref,
                   qs_ref, kaug_ref, s_ref, p_ref, diag_ref, smax_ref, m_ref, l_ref, alpha_ref,
                   acc_ref,
                   *, tile, lambda_init):
    h = pl.program_id(1)
    t = pl.program_id(2)
    n = 2 * tile
    lanes = HEAD_W
    group = tile
    n_aug = 2 * len(LOG2_E_BF16)
    inv_slope = jnp.left_shift(1, jnp.full((1, lanes), h + 1, jnp.int32)).astype(F32)

    @pl.when(t == 0)
    def _():
        key = lax.broadcasted_iota(jnp.int32, (tile, lanes), 0)
        lane = lax.broadcasted_iota(jnp.int32, (tile, lanes), 1)
        lo = (key & 255).astype(F32) / inv_slope
        hi = (key & 256).astype(F32) / inv_slope
        kaug_ref[...] = jnp.where(lane < n_aug // 2, lo,
                                  jnp.where(lane < n_aug, hi, 0.0)).astype(BF16)
        row = lax.broadcasted_iota(jnp.int32, (HEAD_W, n), 0)
        consts = jnp.zeros((HEAD_W, n), F32)
        for i, c in enumerate(LOG2_E_BF16 + LOG2_E_BF16):
            consts = jnp.where(row == i, c, consts)
        qs_ref[HEAD_W:, :] = consts.astype(BF16)
        for c in range(tile // lanes):
            qry = lax.broadcasted_iota(jnp.int32, (tile, lanes), 1) + c * lanes
            future = jnp.minimum(2 * (qry - key), 0).astype(F32) / inv_slope * LOG2_E
            allowed = jnp.right_shift(key, 6) <= jnp.right_shift(qry, 6)
            diag_ref[c, :tile, :] = jnp.where(allowed, future, NEG_INF)

    qt = qt_ref[0, 0, 0]
    first = lax.broadcasted_iota(jnp.int32, (HEAD_W, tile), 0) < HEAD_W // 2
    zero = jnp.zeros_like(qt)
    qs_ref[:HEAD_W, :tile] = jnp.where(first, qt, zero)
    qs_ref[:HEAD_W, tile:] = jnp.where(first, zero, qt)

    m_ref[...] = jnp.full(m_ref.shape, NEG_INF, F32)
    l_ref[...] = jnp.zeros(l_ref.shape, F32)
    acc_ref[...] = jnp.zeros(acc_ref.shape, F32)

    def issue_scores(j, slot):
        start = pl.multiple_of(j * tile, tile)
        keys = jnp.concatenate([k_ref[0, pl.ds(start, tile), :], kaug_ref[...]], axis=1)
        for g in range(n // group):
            gcols = slice(g * group, (g + 1) * group)
            s = _mm(keys, qs_ref[:, gcols])
            for c in range(group // lanes):
                s_ref[slot, g * (group // lanes) + c, :tile, :] = s[:, c * lanes:(c + 1) * lanes]
            smax_ref[slot, :, gcols] = jnp.max(s, axis=0, keepdims=True)

    def softmax(slot, bias, shift):
        for c in range(n // lanes):
            cols = slice(c * lanes, (c + 1) * lanes)
            if bias is None:
                smax = smax_ref[slot, :, cols]
            else:
                sc = s_ref[slot, c, :tile, :] + bias(c)
                s_ref[slot, c, :tile, :] = sc
                smax = jnp.max(sc, axis=0, keepdims=True)
            m_old = m_ref[:, cols]
            m_new = jnp.maximum(m_old, smax + shift)
            alpha = jnp.exp2(m_old - m_new)
            e = jnp.exp2(s_ref[slot, c, :tile, :] - (m_new - shift))
            l_ref[:, cols] = alpha * l_ref[:, cols] + jnp.sum(e, axis=0, keepdims=True)
            m_ref[:, cols] = m_new
            p_ref[slot, c, :tile, :] = e.astype(BF16)
            alpha_ref[slot, :, cols] = alpha

    def values(j, slot):
        p = jnp.concatenate([p_ref[slot, c, :tile, :] for c in range(n // lanes)], axis=1)
        acc_ref[...] = alpha_ref[slot] * acc_ref[...] + _mm(vt_ref[0, 0, j], p)

    def past_shift(i):
        return (jnp.full((1, lanes), (i - t) * tile, jnp.int32).astype(F32) / inv_slope
                * sum(LOG2_E_BF16))

    def diag_bias(c):
        return diag_ref[c % (tile // lanes), :tile, :]

    def step(i, slot, bias, shift, last):
        values(jnp.maximum(i - 1, 0), 1 - slot)
        if not last:
            issue_scores(i + 1, 1 - slot)
        softmax(slot, bias, shift)

    p_ref[1] = jnp.zeros(p_ref.shape[1:], BF16)
    alpha_ref[1] = jnp.ones((1, n), F32)
    issue_scores(0, 0)

    def past_blocks(start, count):
        for k in range(count):
            step(start + k, k % 2, None, past_shift(start + k), False)

    def unrolled_past_blocks(u, carry):
        past_blocks(ATTN_C_UNROLL * u, ATTN_C_UNROLL)
        return carry

    lax.fori_loop(0, t // ATTN_C_UNROLL, unrolled_past_blocks, 0)
    no_shift = jnp.zeros((1, lanes), F32)

    for left in range(ATTN_C_UNROLL):
        @pl.when(t % ATTN_C_UNROLL == left)
        def _(left=left):
            past_blocks(t - left, left)
            step(t, left % 2, diag_bias, no_shift, True)
            values(t, left % 2)

    o_t = acc_ref[...] * (1.0 / l_ref[...])
    lam = lam_ref[...]
    lam_full = (jnp.exp(jnp.sum(lam[0:1] * lam[1:2], axis=-1, keepdims=True))
                - jnp.exp(jnp.sum(lam[2:3] * lam[3:4], axis=-1, keepdims=True)) + lambda_init)
    a = (o_t[:, :tile] - lam_full * o_t[:, tile:]).T
    o_ref[0] = (_rms(a, subg_ref[...]) * (1.0 - lambda_init)).astype(o_ref.dtype)


def _attn_c(q_t, k, v_t, lam, subln_g, lambda_init):
    b, s, _ = k.shape
    tile = ATTN_C_TILE
    n = 2 * tile
    return pl.pallas_call(
        functools.partial(_attn_c_kernel, tile=tile, lambda_init=lambda_init),
        grid=(b, HEADS_C, s // tile),
        in_specs=[
            _resident(lam),
            _resident(subln_g),
            pl.BlockSpec((1, 1, 1, HEAD_W, tile), lambda bi, h, t: (bi, h, t, 0, 0)),
            pl.BlockSpec((1, s, HEAD_W), lambda bi, h, t: (bi, 0, h)),
            pl.BlockSpec((1, 1, s // tile, HEAD_W, tile), lambda bi, h, t: (bi, h, 0, 0, 0)),
        ],
        out_specs=pl.BlockSpec((1, tile, HEAD_W), lambda bi, h, t: (bi, t, h)),
        out_shape=jax.ShapeDtypeStruct((b, s, D_MODEL), BF16),
        scratch_shapes=[
            pltpu.VMEM((2 * HEAD_W, n), BF16),
            pltpu.VMEM((tile, HEAD_W), BF16),
            pltpu.VMEM((2, n // HEAD_W, tile + _pad_rows(F32), HEAD_W), F32),
            pltpu.VMEM((2, n // HEAD_W, tile + _pad_rows(BF16), HEAD_W), BF16),
            pltpu.VMEM((tile // HEAD_W, tile + _pad_rows(F32), HEAD_W), F32),
            pltpu.VMEM((2, 1, n), F32),
            pltpu.VMEM((1, n), F32),
            pltpu.VMEM((1, n), F32),
            pltpu.VMEM((2, 1, n), F32),
            pltpu.VMEM((HEAD_W, n), F32),
        ],
        compiler_params=_params(3),
        name="attn_c",
    )(*_operands(lam, subln_g), q_t, k, v_t)


def kernel(x, norm_g, ff1_w_in, ff1_w_out, ff2_w_in, ff2_w_out, a_w_qkv, a_rel_bias, a_w_o,
           b_w_in, b_ln_g, b_ln_b, b_w_s, b_b_s, b_w_o, c_w_qkv, c_lambda, c_subln_g, c_w_o):
    b, s, d = x.shape
    depth = norm_g.shape[0]
    ff1_w_in, ff1_w_out, ff2_w_in, ff2_w_out, a_w_o, b_w_in, b_w_s, b_w_o, c_w_o = (
        w.astype(BF16) for w in
        (ff1_w_in, ff1_w_out, ff2_w_in, ff2_w_out, a_w_o, b_w_in, b_w_s, b_w_o, c_w_o))

    def split_qkv(w):
        w = w.astype(BF16)
        return w[:, d:2 * d], jnp.concatenate([w[:, :d], w[:, 2 * d:]], axis=1).T

    b_ln_g, b_ln_b, c_subln_g = b_ln_g[:, None], b_ln_b[:, None], c_subln_g[:, None]
    b_b_s_t = jnp.swapaxes(b_b_s, 1, 2)
    xf = x.reshape(b * s, d)
    for i in range(depth):
        g = _Layer(norm_g, i)
        xf = _ffn(xf, g, _Layer(ff1_w_in, i), _Layer(ff1_w_out, i), 0)
        ff2 = (_Layer(ff2_w_in, i), _Layer(ff2_w_out, i), 4)
        kind, j = i % N_MIXERS, i // N_MIXERS
        if kind == 0:
            w_k, w_qv_t = split_qkv(a_w_qkv[j])
            q_t, k, v_t = _norm_proj_t(xf, g, w_k, w_qv_t, 2, b, ATTN_A_TQ, ATTN_A_TQ)
            bias = _attn_a_bias(a_rel_bias[j], ATTN_A_TQ, ATTN_A_TQ + LEFT)
            o = _attn_a(q_t, k.reshape(b, s, d), v_t, bias)
            xf = _proj_ffn(o.reshape(b * s, d), _Layer(a_w_o, j), xf, g, *ff2)
        elif kind == 1:
            u, v = _gmlp_in(xf, g, _Layer(b_w_in, j), _Layer(b_ln_g, j), _Layer(b_ln_b, j), 2)
            xf = _gmlp_ffn(u, v, _Layer(b_w_s, j), _Layer(b_b_s_t, j), _Layer(b_w_o, j), xf, g, *ff2)
        else:
            lambda_init = 0.8 - 0.6 * math.exp(-0.3 * i)
            w_k, w_qv_t = split_qkv(c_w_qkv[j])
            q_t, k, v_t = _norm_proj_t(xf, g, w_k, w_qv_t, 2, b, ATTN_C_TILE, ATTN_C_TILE)
            o = _attn_c(q_t, k.reshape(b, s, d), v_t, _Layer(c_lambda, j), _Layer(c_subln_g, j),
                        lambda_init)
            xf = _proj_ffn(o.reshape(b * s, d), _Layer(c_w_o, j), xf, g, *ff2)
    return xf.reshape(b, s, d)
```

```python
import functools
import math
from typing import NamedTuple

import jax
import jax.numpy as jnp
import numpy as np
from jax import lax
from jax.experimental import pallas as pl
from jax.experimental.pallas import tpu as pltpu

F32 = jnp.float32
BF16 = jnp.bfloat16

D_MODEL = 1024
D_FF = 2816
EPS = 1e-6
NEG_INF = -1e30
N_MIXERS = 3

CHUNK = 64
LEFT = 8 * CHUNK
HEADS_A = 16
REL_CLIP = 128
GMLP_CHUNK = 128
GMLP_GROUPS = 8
HEADS_C = 8
HEAD_W = 128
LOG2_E = math.log2(math.e)
LOG2_E_BF16 = (1.4453125, -0.00262451171875, 7.063150405883789e-06)

V7X_VMEM_LIMIT_BYTES = 56 * 1024 * 1024
V7X_VREG_BYTES = 8 * 128 * 4


def _pad_rows(dtype):
    return V7X_VREG_BYTES // (128 * jnp.dtype(dtype).itemsize)

ROW_TILE = 512
QKV_ROW_TILE = 1024
FFN_ROW_TILE = 1024
PROJ_FFN_ROW_TILE = 1024
FFN_SUB = 256
ATTN_A_TQ = 128
ATTN_A_HEADS = 16
ATTN_C_TILE = 512
ATTN_C_UNROLL = 2


def _params(n_axes):
    return pltpu.CompilerParams(
        dimension_semantics=("arbitrary",) * n_axes,
        vmem_limit_bytes=V7X_VMEM_LIMIT_BYTES,
    )


class _Layer(NamedTuple):
    stack: jax.Array
    index: int


def _resident(p):
    if isinstance(p, _Layer):
        tail = p.stack.shape[1:]
        return pl.BlockSpec((None,) + tail, lambda *_: (p.index,) + (0,) * len(tail),
                            pipeline_mode=pl.Buffered(1))
    return pl.BlockSpec(p.shape, lambda *_: (0,) * p.ndim, pipeline_mode=pl.Buffered(1))


def _operands(*params):
    return [p.stack if isinstance(p, _Layer) else p for p in params]


def _rms(x, g):
    return x * lax.rsqrt(jnp.mean(x * x, axis=-1, keepdims=True) + EPS) * g


def _mm(a, b):
    return jnp.dot(a, b, preferred_element_type=F32)


def _mm_nt(a, b):
    return lax.dot_general(a, b, (((1,), (1,)), ((), ())), preferred_element_type=F32)


def _ffn_chains(x_ref, g, win_ref, wout_ref, o_ref, g_row):
    chains = [slice(r * FFN_SUB, (r + 1) * FFN_SUB) for r in range(x_ref.shape[0] // FFN_SUB)]

    def up_proj(rows):
        return _mm(_rms(x_ref[rows, :], g[g_row:g_row + 1]).astype(BF16), win_ref[...])

    def down_proj(gu):
        gate = gu[:, :D_FF]
        up = gu[:, D_FF:]
        h = (gate * (1.0 / (1.0 + jnp.exp(-gate))) * up).astype(BF16)
        return _mm(h, wout_ref[...])

    gu_next = up_proj(chains[0])
    y_prev = None
    for r in range(len(chains)):
        gu = gu_next
        if r + 1 < len(chains):
            gu_next = up_proj(chains[r + 1])
        y = down_proj(gu)
        if y_prev is not None:
            o_ref[chains[r - 1], :] = (x_ref[chains[r - 1], :]
                                       + 0.5 * _rms(y_prev, g[g_row + 1:g_row + 2]))
        y_prev = y
    o_ref[chains[-1], :] = x_ref[chains[-1], :] + 0.5 * _rms(y_prev, g[g_row + 1:g_row + 2])


def _ffn_kernel(x_ref, g_ref, win_ref, wout_ref, o_ref, *, g_row):
    _ffn_chains(x_ref, g_ref[...], win_ref, wout_ref, o_ref, g_row)


def _proj_ffn_kernel(a_ref, wo_ref, x_ref, g_ref, win_ref, wout_ref, o_ref, x2_ref, *, g_row):
    g = g_ref[...]
    for r in range(x_ref.shape[0] // FFN_SUB):
        rows = slice(r * FFN_SUB, (r + 1) * FFN_SUB)
        m = _mm(a_ref[rows, :], wo_ref[...])
        x2_ref[rows, :] = x_ref[rows, :] + _rms(m, g[g_row - 1:g_row])
    _ffn_chains(x2_ref, g, win_ref, wout_ref, o_ref, g_row)


def _proj_ffn(a, w_o, x, g, w_in, w_out, g_row):
    t, d = x.shape
    rows = pl.BlockSpec((PROJ_FFN_ROW_TILE, d), lambda i: (i, 0))
    return pl.pallas_call(
        functools.partial(_proj_ffn_kernel, g_row=g_row),
        grid=(t // PROJ_FFN_ROW_TILE,),
        in_specs=[rows, _resident(w_o), rows, _resident(g), _resident(w_in), _resident(w_out)],
        out_specs=rows,
        out_shape=jax.ShapeDtypeStruct((t, d), F32),
        scratch_shapes=[pltpu.VMEM((PROJ_FFN_ROW_TILE, d), F32)],
        compiler_params=_params(1),
        name="proj_ffn",
    )(a, *_operands(w_o), x, *_operands(g, w_in, w_out))


def _ffn(x, g, w_in, w_out, g_row):
    t, d = x.shape
    return pl.pallas_call(
        functools.partial(_ffn_kernel, g_row=g_row),
        grid=(t // FFN_ROW_TILE,),
        in_specs=[
            pl.BlockSpec((FFN_ROW_TILE, d), lambda i: (i, 0)),
            _resident(g),
            _resident(w_in),
            _resident(w_out),
        ],
        out_specs=pl.BlockSpec((FFN_ROW_TILE, d), lambda i: (i, 0)),
        out_shape=jax.ShapeDtypeStruct((t, d), F32),
        compiler_params=_params(1),
        name="ffn",
    )(x, *_operands(g, w_in, w_out))


N_GROUPS = D_MODEL // HEAD_W


def _norm_proj_t_kernel(x_ref, g_ref, wk_ref, wqv_ref, qt_ref, k_ref, vt_ref, *, g_row, tq, tv):
    chain = ROW_TILE
    chains = [slice(r * chain, (r + 1) * chain) for r in range(x_ref.shape[0] // chain)]
    xns = [_rms(x_ref[rows, :], g_ref[g_row:g_row + 1, :]).astype(BF16) for rows in chains]
    for r, (rows, xn) in enumerate(zip(chains, xns)):
        k_ref[rows, :] = _mm(xn, wk_ref[...]).astype(k_ref.dtype)
        qv_t = _mm_nt(wqv_ref[...], xn)
        q_t = (qv_t[:D_MODEL] * ((HEAD_W // 2) ** -0.5 * LOG2_E)).astype(BF16)
        v_t = qv_t[D_MODEL:].astype(BF16)
        for part in range(chain // tq):
            tile = q_t[:, part * tq:(part + 1) * tq].reshape(N_GROUPS, HEAD_W, tq)
            qt_ref[0, :, r * (chain // tq) + part] = tile
        for part in range(chain // tv):
            tile = v_t[:, part * tv:(part + 1) * tv].reshape(N_GROUPS, HEAD_W, tv)
            vt_ref[0, :, r * (chain // tv) + part] = tile


def _norm_proj_t(x, g, w_k, w_qv_t, g_row, batch, tq, tv):
    t, d = x.shape
    s = t // batch
    rows = QKV_ROW_TILE
    per_batch = s // rows

    def tiles(width):
        return pl.BlockSpec((1, N_GROUPS, rows // width, HEAD_W, width),
                            lambda i: (i // per_batch, 0, i % per_batch, 0, 0))

    return pl.pallas_call(
        functools.partial(_norm_proj_t_kernel, g_row=g_row, tq=tq, tv=tv),
        grid=(t // rows,),
        in_specs=[
            pl.BlockSpec((rows, d), lambda i: (i, 0)),
            _resident(g),
            _resident(w_k),
            _resident(w_qv_t),
        ],
        out_specs=[tiles(tq), pl.BlockSpec((rows, d), lambda i: (i, 0)), tiles(tv)],
        out_shape=[
            jax.ShapeDtypeStruct((batch, N_GROUPS, s // tq, HEAD_W, tq), BF16),
            jax.ShapeDtypeStruct((t, d), BF16),
            jax.ShapeDtypeStruct((batch, N_GROUPS, s // tv, HEAD_W, tv), BF16),
        ],
        compiler_params=_params(1),
        name="norm_proj_t",
    )(x, *_operands(g, w_k, w_qv_t))


def _attn_a_bias(rel_bias, tq, win):
    heads = rel_bias.shape[0]
    tbl = rel_bias.astype(F32)
    f = jnp.concatenate([
        jnp.broadcast_to(tbl[:, 2 * REL_CLIP:], (heads, tq + LEFT - REL_CLIP)),
        tbl[:, 2 * REL_CLIP - 1:0:-1],
    ], axis=1)
    p = win + tq
    skew = jnp.tile(jnp.pad(f, ((0, 0), (0, 1))), (1, tq))[:, :tq * (p - 1)]
    bias = skew.reshape(heads, tq, p - 1)[:, :, tq - 1:tq - 1 + win]
    r = np.arange(tq)[:, None] // CHUNK
    j = np.arange(win)[None, :]
    valid = (j >= r * CHUNK) & (j < (r + LEFT // CHUNK + 1) * CHUNK)
    bias = jnp.where(valid, bias * LOG2_E, NEG_INF)
    bias = jnp.pad(bias, ((0, 0), (0, 0), (0, LEFT)), constant_values=NEG_INF)
    bias = bias.reshape(HEADS_A // 2, 2, tq, win + LEFT)
    return jnp.transpose(bias, (0, 3, 1, 2)).reshape(HEADS_A // 2, win + LEFT, 2 * tq)


def _attn_a_kernel(qt_ref, k_ref, vt_ref, bias_ref, o_ref, *, tq, win):
    t = pl.program_id(2)
    wb = jnp.maximum(t - LEFT // tq, 0)
    ws = pl.multiple_of(wb * tq, tq)
    shift = pl.multiple_of(jnp.maximum(LEFT - t * tq, 0), tq)
    first = lax.broadcasted_iota(jnp.int32, (HEAD_W, tq), 0) < HEAD_W // 2
    pairs = ATTN_A_HEADS // 2

    def scores(pp):
        zero = jnp.zeros((HEAD_W, tq), BF16)
        blocks = []
        for p in (2 * pp, 2 * pp + 1):
            qt = qt_ref[0, p, 0]
            row = [zero] * 4
            row[2 * (p % 2)] = jnp.where(first, qt, zero)
            row[2 * (p % 2) + 1] = jnp.where(first, zero, qt)
            blocks.append(jnp.concatenate(row, axis=1))
        qs = jnp.concatenate(blocks, axis=0)
        keys = k_ref[0, pl.ds(ws, win), 2 * pp * HEAD_W:(2 * pp + 2) * HEAD_W]
        bias = jnp.concatenate([bias_ref[2 * pp, pl.ds(shift, win), :],
                                bias_ref[2 * pp + 1, pl.ds(shift, win), :]], axis=1)
        return _mm(keys, qs) + bias

    def attend(p, s):
        m = jnp.max(s, axis=0, keepdims=True)
        e = jnp.exp2(s - m)
        l = jnp.sum(e, axis=0, keepdims=True)
        v_t = jnp.concatenate([vt_ref[0, p, wb + c] for c in range(win // tq)], axis=1)
        o_t = _mm(v_t, e.astype(BF16)) * (1.0 / l)
        o_t = jnp.where(first, o_t[:, :tq], o_t[:, tq:])
        o_ref[0, :, p * HEAD_W:(p + 1) * HEAD_W] = o_t.T.astype(o_ref.dtype)

    s_next = scores(0)
    for pp in range(pairs // 2):
        s_cur = s_next
        if pp + 1 < pairs // 2:
            s_next = scores(pp + 1)
        attend(2 * pp, s_cur[:, :2 * tq])
        attend(2 * pp + 1, s_cur[:, 2 * tq:])


def _attn_a(q_t, k, v_t, bias):
    b, s, _ = k.shape
    tq = ATTN_A_TQ
    win = tq + LEFT
    pairs = ATTN_A_HEADS // 2
    gw = pairs * HEAD_W
    assert bias.shape == (HEADS_A // 2, win + LEFT, 2 * tq)
    return pl.pallas_call(
        functools.partial(_attn_a_kernel, tq=tq, win=win),
        grid=(b, D_MODEL // gw, s // tq),
        in_specs=[
            pl.BlockSpec((1, pairs, 1, HEAD_W, tq), lambda bi, hg, t: (bi, hg, t, 0, 0)),
            pl.BlockSpec((1, s, gw), lambda bi, hg, t: (bi, 0, hg), pipeline_mode=pl.Buffered(1)),
            pl.BlockSpec((1, pairs, s // tq, HEAD_W, tq), lambda bi, hg, t: (bi, hg, 0, 0, 0),
                         pipeline_mode=pl.Buffered(1)),
            pl.BlockSpec((pairs, win + LEFT, 2 * tq), lambda bi, hg, t: (hg, 0, 0),
                         pipeline_mode=pl.Buffered(1)),
        ],
        out_specs=pl.BlockSpec((1, tq, gw), lambda bi, hg, t: (bi, t, hg)),
        out_shape=jax.ShapeDtypeStruct((b, s, D_MODEL), BF16),
        compiler_params=_params(3),
        name="attn_a",
    )(q_t, k, v_t, bias)


def _gmlp_in_kernel(x_ref, g_ref, w_ref, lng_ref, lnb_ref, u_ref, v_ref, *, g_row):
    chains = [slice(r * FFN_SUB, (r + 1) * FFN_SUB) for r in range(x_ref.shape[0] // FFN_SUB)]
    hws = [_mm(_rms(x_ref[rows, :], g_ref[g_row:g_row + 1, :]).astype(BF16), w_ref[...])
           for rows in chains]
    for rows, hw in zip(chains, hws):
        act = 0.5 * hw * (1.0 + lax.erf(hw * math.sqrt(0.5)))
        u_ref[rows, :] = act[:, :D_MODEL].astype(u_ref.dtype)
        v = act[:, D_MODEL:]
        mu = jnp.mean(v, axis=-1, keepdims=True)
        vc = v - mu
        var = jnp.mean(vc * vc, axis=-1, keepdims=True)
        v_ref[rows, :] = (vc * lax.rsqrt(var + EPS) * lng_ref[...] + lnb_ref[...]).astype(v_ref.dtype)


def _gmlp_in(x, g, w, ln_g, ln_b, g_row):
    t, d = x.shape
    out = jax.ShapeDtypeStruct((t, d), BF16)
    row_spec = pl.BlockSpec((ROW_TILE, d), lambda i: (i, 0))
    return pl.pallas_call(
        functools.partial(_gmlp_in_kernel, g_row=g_row),
        grid=(t // ROW_TILE,),
        in_specs=[row_spec, _resident(g), _resident(w),
                  _resident(ln_g), _resident(ln_b)],
        out_specs=[row_spec, row_spec],
        out_shape=[out, out],
        compiler_params=_params(1),
        name="gmlp_in",
    )(x, *_operands(g, w, ln_g, ln_b))


def _gmlp_ffn_kernel(u_ref, v_ref, ws_ref, bs_ref, wo_ref, x_ref, g_ref, win_ref, wout_ref,
                     out_ref, y_ref, x2_ref, *, g_row):
    c = GMLP_CHUNK
    row = lax.broadcasted_iota(jnp.int32, (c, c), 0)
    col = lax.broadcasted_iota(jnp.int32, (c, c), 1)
    causal = row >= col
    for grp in range(GMLP_GROUPS):
        w = ws_ref[grp]
        w = jnp.where(causal, w, jnp.zeros_like(w))
        b = bs_ref[:, grp:grp + 1]
        lanes = slice(grp * c, (grp + 1) * c)
        for blk in range(x_ref.shape[0] // c):
            rows = slice(blk * c, (blk + 1) * c)
            sv = _mm(w, v_ref[rows, lanes]) + b
            y_ref[rows, lanes] = (u_ref[rows, lanes].astype(F32) * sv).astype(y_ref.dtype)
    g = g_ref[...]
    for r in range(x_ref.shape[0] // FFN_SUB):
        rows = slice(r * FFN_SUB, (r + 1) * FFN_SUB)
        m = _mm(y_ref[rows, :], wo_ref[...])
        x2_ref[rows, :] = x_ref[rows, :] + _rms(m, g[g_row - 1:g_row])
    _ffn_chains(x2_ref, g, win_ref, wout_ref, out_ref, g_row)


def _gmlp_ffn(u, v, w_s, b_s_t, w_o, x, g, w_in, w_out, g_row):
    t, d = x.shape
    row_spec = pl.BlockSpec((ROW_TILE, d), lambda i: (i, 0))
    return pl.pallas_call(
        functools.partial(_gmlp_ffn_kernel, g_row=g_row),
        grid=(t // ROW_TILE,),
        in_specs=[row_spec, row_spec, _resident(w_s), _resident(b_s_t), _resident(w_o), row_spec,
                  _resident(g), _resident(w_in), _resident(w_out)],
        out_specs=row_spec,
        out_shape=jax.ShapeDtypeStruct((t, d), F32),
        scratch_shapes=[pltpu.VMEM((ROW_TILE, d), BF16), pltpu.VMEM((ROW_TILE, d), F32)],
        compiler_params=_params(1),
        name="gmlp_ffn",
    )(u, v, *_operands(w_s, b_s_t, w_o), x, *_operands(g, w_in, w_out))


def _attn_c_kernel(lam_ref, subg_ref, qt_ref, k_ref, vt_ref, o_ref,
                   qs_ref, kaug_ref, s_ref, p_ref, diag_ref, smax_ref, m_ref, alpha_ref, acc_ref,
                   *, tile, lambda_init):
    h = pl.program_id(1)
    t = pl.program_id(2)
    n = 2 * tile
    lanes = HEAD_W
    group = tile
    n_aug = 2 * len(LOG2_E_BF16)
    inv_slope = jnp.left_shift(1, jnp.full((1, lanes), h + 1, jnp.int32)).astype(F32)

    @pl.when(t == 0)
    def _():
        key = lax.broadcasted_iota(jnp.int32, (tile, lanes), 0)
        lane = lax.broadcasted_iota(jnp.int32, (tile, lanes), 1)
        lo = (key & 255).astype(F32) / inv_slope
        hi = (key & 256).astype(F32) / inv_slope
        kaug_ref[...] = jnp.where(lane < n_aug // 2, lo,
                                  jnp.where(lane < n_aug, hi, 0.0)).astype(BF16)
        row = lax.broadcasted_iota(jnp.int32, (HEAD_W, n), 0)
        consts = jnp.zeros((HEAD_W, n), F32)
        for i, c in enumerate(LOG2_E_BF16 + LOG2_E_BF16):
            consts = jnp.where(row == i, c, consts)
        qs_ref[HEAD_W:, :] = consts.astype(BF16)
        for c in range(tile // lanes):
            qry = lax.broadcasted_iota(jnp.int32, (tile, lanes), 1) + c * lanes
            future = jnp.minimum(2 * (qry - key), 0).astype(F32) / inv_slope * LOG2_E
            allowed = jnp.right_shift(key, 6) <= jnp.right_shift(qry, 6)
            diag_ref[c, :tile, :] = jnp.where(allowed, future, NEG_INF)

    qt = qt_ref[0, 0, 0]
    first = lax.broadcasted_iota(jnp.int32, (HEAD_W, tile), 0) < HEAD_W // 2
    zero = jnp.zeros_like(qt)
    qs_ref[:HEAD_W, :tile] = jnp.where(first, qt, zero)
    qs_ref[:HEAD_W, tile:] = jnp.where(first, zero, qt)

    m_ref[...] = jnp.full(m_ref.shape, NEG_INF, F32)
    acc_ref[...] = jnp.zeros(acc_ref.shape, F32)

    def issue_scores(j, slot):
        start = pl.multiple_of(j * tile, tile)
        keys = jnp.concatenate([k_ref[0, pl.ds(start, tile), :], kaug_ref[...]], axis=1)
        for g in range(n // group):
            gcols = slice(g * group, (g + 1) * group)
            s = _mm(keys, qs_ref[:, gcols])
            for c in range(group // lanes):
                s_ref[slot, g * (group // lanes) + c, :tile, :] = s[:, c * lanes:(c + 1) * lanes]
            smax_ref[slot, :, gcols] = jnp.max(s, axis=0, keepdims=True)

    def softmax(slot, bias, shift):
        for c in range(n // lanes):
            cols = slice(c * lanes, (c + 1) * lanes)
            keys = tile if bias is None else (c % (tile // lanes) + 1) * lanes
            if bias is None:
                smax = smax_ref[slot, :, cols]
            else:
                sc = s_ref[slot, c, :keys, :] + bias(c)[:keys]
                s_ref[slot, c, :keys, :] = sc
                smax = jnp.max(sc, axis=0, keepdims=True)
                if keys < tile:
                    p_ref[slot, c, keys:tile, :] = jnp.zeros((tile - keys, lanes), BF16)
            m_old = m_ref[:, cols]
            m_new = jnp.maximum(m_old, smax + shift)
            alpha = jnp.exp2(m_old - m_new)
            e = jnp.exp2(s_ref[slot, c, :keys, :] - (m_new - shift))
            m_ref[:, cols] = m_new
            p_ref[slot, c, :keys, :] = e.astype(BF16)
            alpha_ref[slot, :, cols] = alpha

    ones_rows = jnp.ones((_pad_rows(BF16), tile), BF16)

    def values(j, slot):
        p = jnp.concatenate([p_ref[slot, c, :tile, :] for c in range(n // lanes)], axis=1)
        v_t = jnp.concatenate([vt_ref[0, 0, j], ones_rows], axis=0)
        acc_ref[...] = alpha_ref[slot] * acc_ref[...] + _mm(v_t, p)

    def past_shift(i):
        return (jnp.full((1, lanes), (i - t) * tile, jnp.int32).astype(F32) / inv_slope
                * sum(LOG2_E_BF16))

    def diag_bias(c):
        return diag_ref[c % (tile // lanes), :tile, :]

    def step(i, slot, bias, shift, last):
        values(jnp.maximum(i - 1, 0), 1 - slot)
        if not last:
            issue_scores(i + 1, 1 - slot)
        softmax(slot, bias, shift)

    p_ref[1] = jnp.zeros(p_ref.shape[1:], BF16)
    alpha_ref[1] = jnp.ones((1, n), F32)
    issue_scores(0, 0)

    def past_blocks(start, count):
        for k in range(count):
            step(start + k, k % 2, None, past_shift(start + k), False)

    def unrolled_past_blocks(u, carry):
        past_blocks(ATTN_C_UNROLL * u, ATTN_C_UNROLL)
        return carry

    lax.fori_loop(0, t // ATTN_C_UNROLL, unrolled_past_blocks, 0)
    no_shift = jnp.zeros((1, lanes), F32)

    for left in range(ATTN_C_UNROLL):
        @pl.when(t % ATTN_C_UNROLL == left)
        def _(left=left):
            past_blocks(t - left, left)
            step(t, left % 2, diag_bias, no_shift, True)
            values(t, left % 2)

    o_t = acc_ref[:HEAD_W, :] * (1.0 / acc_ref[HEAD_W:HEAD_W + 1, :])
    lam = lam_ref[...]
    lam_full = (jnp.exp(jnp.sum(lam[0:1] * lam[1:2], axis=-1, keepdims=True))
                - jnp.exp(jnp.sum(lam[2:3] * lam[3:4], axis=-1, keepdims=True)) + lambda_init)
    a = (o_t[:, :tile] - lam_full * o_t[:, tile:]).T
    o_ref[0] = (_rms(a, subg_ref[...]) * (1.0 - lambda_init)).astype(o_ref.dtype)


def _attn_c(q_t, k, v_t, lam, subln_g, lambda_init):
    b, s, _ = k.shape
    tile = ATTN_C_TILE
    n = 2 * tile
    return pl.pallas_call(
        functools.partial(_attn_c_kernel, tile=tile, lambda_init=lambda_init),
        grid=(b, HEADS_C, s // tile),
        in_specs=[
            _resident(lam),
            _resident(subln_g),
            pl.BlockSpec((1, 1, 1, HEAD_W, tile), lambda bi, h, t: (bi, h, t, 0, 0)),
            pl.BlockSpec((1, s, HEAD_W), lambda bi, h, t: (bi, 0, h)),
            pl.BlockSpec((1, 1, s // tile, HEAD_W, tile), lambda bi, h, t: (bi, h, 0, 0, 0)),
        ],
        out_specs=pl.BlockSpec((1, tile, HEAD_W), lambda bi, h, t: (bi, t, h)),
        out_shape=jax.ShapeDtypeStruct((b, s, D_MODEL), BF16),
        scratch_shapes=[
            pltpu.VMEM((2 * HEAD_W, n), BF16),
            pltpu.VMEM((tile, HEAD_W), BF16),
            pltpu.VMEM((2, n // HEAD_W, tile + _pad_rows(F32), HEAD_W), F32),
            pltpu.VMEM((2, n // HEAD_W, tile + _pad_rows(BF16), HEAD_W), BF16),
            pltpu.VMEM((tile // HEAD_W, tile + _pad_rows(F32), HEAD_W), F32),
            pltpu.VMEM((2, 1, n), F32),
            pltpu.VMEM((1, n), F32),
            pltpu.VMEM((2, 1, n), F32),
            pltpu.VMEM((HEAD_W + _pad_rows(BF16), n), F32),
        ],
        compiler_params=_params(3),
        name="attn_c",
    )(*_operands(lam, subln_g), q_t, k, v_t)


def kernel(x, norm_g, ff1_w_in, ff1_w_out, ff2_w_in, ff2_w_out, a_w_qkv, a_rel_bias, a_w_o,
           b_w_in, b_ln_g, b_ln_b, b_w_s, b_b_s, b_w_o, c_w_qkv, c_lambda, c_subln_g, c_w_o):
    b, s, d = x.shape
    depth = norm_g.shape[0]
    ff1_w_in, ff1_w_out, ff2_w_in, ff2_w_out, a_w_o, b_w_in, b_w_s, b_w_o, c_w_o = (
        w.astype(BF16) for w in
        (ff1_w_in, ff1_w_out, ff2_w_in, ff2_w_out, a_w_o, b_w_in, b_w_s, b_w_o, c_w_o))

    def split_qkv(w):
        w = w.astype(BF16)
        return w[:, d:2 * d], jnp.concatenate([w[:, :d], w[:, 2 * d:]], axis=1).T

    b_ln_g, b_ln_b, c_subln_g = b_ln_g[:, None], b_ln_b[:, None], c_subln_g[:, None]
    b_b_s_t = jnp.swapaxes(b_b_s, 1, 2)
    xf = x.reshape(b * s, d)
    for i in range(depth):
        g = _Layer(norm_g, i)
        xf = _ffn(xf, g, _Layer(ff1_w_in, i), _Layer(ff1_w_out, i), 0)
        ff2 = (_Layer(ff2_w_in, i), _Layer(ff2_w_out, i), 4)
        kind, j = i % N_MIXERS, i // N_MIXERS
        if kind == 0:
            w_k, w_qv_t = split_qkv(a_w_qkv[j])
            q_t, k, v_t = _norm_proj_t(xf, g, w_k, w_qv_t, 2, b, ATTN_A_TQ, ATTN_A_TQ)
            bias = _attn_a_bias(a_rel_bias[j], ATTN_A_TQ, ATTN_A_TQ + LEFT)
            o = _attn_a(q_t, k.reshape(b, s, d), v_t, bias)
            xf = _proj_ffn(o.reshape(b * s, d), _Layer(a_w_o, j), xf, g, *ff2)
        elif kind == 1:
            u, v = _gmlp_in(xf, g, _Layer(b_w_in, j), _Layer(b_ln_g, j), _Layer(b_ln_b, j), 2)
            xf = _gmlp_ffn(u, v, _Layer(b_w_s, j), _Layer(b_b_s_t, j), _Layer(b_w_o, j), xf, g, *ff2)
        else:
            lambda_init = 0.8 - 0.6 * math.exp(-0.3 * i)
            w_k, w_qv_t = split_qkv(c_w_qkv[j])
            q_t, k, v_t = _norm_proj_t(xf, g, w_k, w_qv_t, 2, b, ATTN_C_TILE, ATTN_C_TILE)
            o = _attn_c(q_t, k.reshape(b, s, d), v_t, _Layer(c_lambda, j), _Layer(c_subln_g, j),
                        lambda_init)
            xf = _proj_ffn(o.reshape(b * s, d), _Layer(c_w_o, j), xf, g, *ff2)
    return xf.reshape(b, s, d)
```

```python
import functools
import math
from typing import NamedTuple

import jax
import jax.numpy as jnp
import numpy as np
from jax import lax
from jax.experimental import pallas as pl
from jax.experimental.pallas import tpu as pltpu

F32 = jnp.float32
BF16 = jnp.bfloat16

D_MODEL = 1024
D_FF = 2816
EPS = 1e-6
NEG_INF = -1e30
N_MIXERS = 3

CHUNK = 64
LEFT = 8 * CHUNK
HEADS_A = 16
REL_CLIP = 128
GMLP_CHUNK = 128
GMLP_GROUPS = 8
HEADS_C = 8
HEAD_W = 128
LOG2_E = math.log2(math.e)
LOG2_E_BF16 = (1.4453125, -0.00262451171875, 7.063150405883789e-06)

V7X_VMEM_LIMIT_BYTES = 56 * 1024 * 1024
V7X_VREG_BYTES = 8 * 128 * 4


def _pad_rows(dtype):
    return V7X_VREG_BYTES // (128 * jnp.dtype(dtype).itemsize)

ROW_TILE = 512
QKV_ROW_TILE = 1024
FFN_ROW_TILE = 1024
PROJ_FFN_ROW_TILE = 1024
FFN_SUB = 256
ATTN_A_TQ = 128
ATTN_A_HEADS = 16
ATTN_C_TILE = 512
ATTN_C_HEADS = 2
ATTN_C_UNROLL = 2


def _params(n_axes):
    return pltpu.CompilerParams(
        dimension_semantics=("arbitrary",) * n_axes,
        vmem_limit_bytes=V7X_VMEM_LIMIT_BYTES,
    )


class _Layer(NamedTuple):
    stack: jax.Array
    index: int


def _resident(p):
    if isinstance(p, _Layer):
        tail = p.stack.shape[1:]
        return pl.BlockSpec((None,) + tail, lambda *_: (p.index,) + (0,) * len(tail),
                            pipeline_mode=pl.Buffered(1))
    return pl.BlockSpec(p.shape, lambda *_: (0,) * p.ndim, pipeline_mode=pl.Buffered(1))


def _operands(*params):
    return [p.stack if isinstance(p, _Layer) else p for p in params]


def _rms(x, g):
    return x * lax.rsqrt(jnp.mean(x * x, axis=-1, keepdims=True) + EPS) * g


def _mm(a, b):
    return jnp.dot(a, b, preferred_element_type=F32)


def _mm_nt(a, b):
    return lax.dot_general(a, b, (((1,), (1,)), ((), ())), preferred_element_type=F32)


def _ffn_chains(x_ref, g, win_ref, wout_ref, o_ref, g_row):
    chains = [slice(r * FFN_SUB, (r + 1) * FFN_SUB) for r in range(x_ref.shape[0] // FFN_SUB)]

    def up_proj(rows):
        return _mm(_rms(x_ref[rows, :], g[g_row:g_row + 1]).astype(BF16), win_ref[...])

    def down_proj(gu):
        gate = gu[:, :D_FF]
        up = gu[:, D_FF:]
        h = (gate * (1.0 / (1.0 + jnp.exp(-gate))) * up).astype(BF16)
        return _mm(h, wout_ref[...])

    gu_next = up_proj(chains[0])
    y_prev = None
    for r in range(len(chains)):
        gu = gu_next
        if r + 1 < len(chains):
            gu_next = up_proj(chains[r + 1])
        y = down_proj(gu)
        if y_prev is not None:
            o_ref[chains[r - 1], :] = (x_ref[chains[r - 1], :]
                                       + 0.5 * _rms(y_prev, g[g_row + 1:g_row + 2]))
        y_prev = y
    o_ref[chains[-1], :] = x_ref[chains[-1], :] + 0.5 * _rms(y_prev, g[g_row + 1:g_row + 2])


def _ffn_kernel(x_ref, g_ref, win_ref, wout_ref, o_ref, *, g_row):
    _ffn_chains(x_ref, g_ref[...], win_ref, wout_ref, o_ref, g_row)


def _proj_ffn_kernel(a_ref, wo_ref, x_ref, g_ref, win_ref, wout_ref, o_ref, x2_ref, *, g_row):
    g = g_ref[...]
    for r in range(x_ref.shape[0] // FFN_SUB):
        rows = slice(r * FFN_SUB, (r + 1) * FFN_SUB)
        m = _mm(a_ref[rows, :], wo_ref[...])
        x2_ref[rows, :] = x_ref[rows, :] + _rms(m, g[g_row - 1:g_row])
    _ffn_chains(x2_ref, g, win_ref, wout_ref, o_ref, g_row)


def _proj_ffn(a, w_o, x, g, w_in, w_out, g_row):
    t, d = x.shape
    rows = pl.BlockSpec((PROJ_FFN_ROW_TILE, d), lambda i: (i, 0))
    return pl.pallas_call(
        functools.partial(_proj_ffn_kernel, g_row=g_row),
        grid=(t // PROJ_FFN_ROW_TILE,),
        in_specs=[rows, _resident(w_o), rows, _resident(g), _resident(w_in), _resident(w_out)],
        out_specs=rows,
        out_shape=jax.ShapeDtypeStruct((t, d), F32),
        scratch_shapes=[pltpu.VMEM((PROJ_FFN_ROW_TILE, d), F32)],
        compiler_params=_params(1),
        name="proj_ffn",
    )(a, *_operands(w_o), x, *_operands(g, w_in, w_out))


def _ffn(x, g, w_in, w_out, g_row):
    t, d = x.shape
    return pl.pallas_call(
        functools.partial(_ffn_kernel, g_row=g_row),
        grid=(t // FFN_ROW_TILE,),
        in_specs=[
            pl.BlockSpec((FFN_ROW_TILE, d), lambda i: (i, 0)),
            _resident(g),
            _resident(w_in),
            _resident(w_out),
        ],
        out_specs=pl.BlockSpec((FFN_ROW_TILE, d), lambda i: (i, 0)),
        out_shape=jax.ShapeDtypeStruct((t, d), F32),
        compiler_params=_params(1),
        name="ffn",
    )(x, *_operands(g, w_in, w_out))


N_GROUPS = D_MODEL // HEAD_W


def _norm_proj_t_kernel(x_ref, g_ref, wk_ref, wqv_ref, qt_ref, k_ref, vt_ref, *, g_row, tq, tv):
    chain = ROW_TILE
    chains = [slice(r * chain, (r + 1) * chain) for r in range(x_ref.shape[0] // chain)]
    xns = [_rms(x_ref[rows, :], g_ref[g_row:g_row + 1, :]).astype(BF16) for rows in chains]
    for r, (rows, xn) in enumerate(zip(chains, xns)):
        k_ref[rows, :] = _mm(xn, wk_ref[...]).astype(k_ref.dtype)
        qv_t = _mm_nt(wqv_ref[...], xn)
        q_t = (qv_t[:D_MODEL] * ((HEAD_W // 2) ** -0.5 * LOG2_E)).astype(BF16)
        v_t = qv_t[D_MODEL:].astype(BF16)
        for part in range(chain // tq):
            tile = q_t[:, part * tq:(part + 1) * tq].reshape(N_GROUPS, HEAD_W, tq)
            qt_ref[0, :, r * (chain // tq) + part] = tile
        for part in range(chain // tv):
            tile = v_t[:, part * tv:(part + 1) * tv].reshape(N_GROUPS, HEAD_W, tv)
            vt_ref[0, :, r * (chain // tv) + part] = tile


def _norm_proj_t(x, g, w_k, w_qv_t, g_row, batch, tq, tv):
    t, d = x.shape
    s = t // batch
    rows = QKV_ROW_TILE
    per_batch = s // rows

    def tiles(width):
        return pl.BlockSpec((1, N_GROUPS, rows // width, HEAD_W, width),
                            lambda i: (i // per_batch, 0, i % per_batch, 0, 0))

    return pl.pallas_call(
        functools.partial(_norm_proj_t_kernel, g_row=g_row, tq=tq, tv=tv),
        grid=(t // rows,),
        in_specs=[
            pl.BlockSpec((rows, d), lambda i: (i, 0)),
            _resident(g),
            _resident(w_k),
            _resident(w_qv_t),
        ],
        out_specs=[tiles(tq), pl.BlockSpec((rows, d), lambda i: (i, 0)), tiles(tv)],
        out_shape=[
            jax.ShapeDtypeStruct((batch, N_GROUPS, s // tq, HEAD_W, tq), BF16),
            jax.ShapeDtypeStruct((t, d), BF16),
            jax.ShapeDtypeStruct((batch, N_GROUPS, s // tv, HEAD_W, tv), BF16),
        ],
        compiler_params=_params(1),
        name="norm_proj_t",
    )(x, *_operands(g, w_k, w_qv_t))


def _attn_a_bias(rel_bias, tq, win):
    heads = rel_bias.shape[0]
    tbl = rel_bias.astype(F32)
    f = jnp.concatenate([
        jnp.broadcast_to(tbl[:, 2 * REL_CLIP:], (heads, tq + LEFT - REL_CLIP)),
        tbl[:, 2 * REL_CLIP - 1:0:-1],
    ], axis=1)
    p = win + tq
    skew = jnp.tile(jnp.pad(f, ((0, 0), (0, 1))), (1, tq))[:, :tq * (p - 1)]
    bias = skew.reshape(heads, tq, p - 1)[:, :, tq - 1:tq - 1 + win]
    r = np.arange(tq)[:, None] // CHUNK
    j = np.arange(win)[None, :]
    valid = (j >= r * CHUNK) & (j < (r + LEFT // CHUNK + 1) * CHUNK)
    bias = jnp.where(valid, bias * LOG2_E, NEG_INF)
    bias = jnp.pad(bias, ((0, 0), (0, 0), (0, LEFT)), constant_values=NEG_INF)
    bias = bias.reshape(HEADS_A // 2, 2, tq, win + LEFT)
    return jnp.transpose(bias, (0, 3, 1, 2)).reshape(HEADS_A // 2, win + LEFT, 2 * tq)


def _attn_a_kernel(qt_ref, k_ref, vt_ref, bias_ref, o_ref, *, tq, win):
    t = pl.program_id(2)
    wb = jnp.maximum(t - LEFT // tq, 0)
    ws = pl.multiple_of(wb * tq, tq)
    shift = pl.multiple_of(jnp.maximum(LEFT - t * tq, 0), tq)
    first = lax.broadcasted_iota(jnp.int32, (HEAD_W, tq), 0) < HEAD_W // 2
    pairs = ATTN_A_HEADS // 2

    def scores(pp):
        zero = jnp.zeros((HEAD_W, tq), BF16)
        blocks = []
        for p in (2 * pp, 2 * pp + 1):
            qt = qt_ref[0, p, 0]
            row = [zero] * 4
            row[2 * (p % 2)] = jnp.where(first, qt, zero)
            row[2 * (p % 2) + 1] = jnp.where(first, zero, qt)
            blocks.append(jnp.concatenate(row, axis=1))
        qs = jnp.concatenate(blocks, axis=0)
        keys = k_ref[0, pl.ds(ws, win), 2 * pp * HEAD_W:(2 * pp + 2) * HEAD_W]
        bias = jnp.concatenate([bias_ref[2 * pp, pl.ds(shift, win), :],
                                bias_ref[2 * pp + 1, pl.ds(shift, win), :]], axis=1)
        return _mm(keys, qs) + bias

    def attend(p, s):
        m = jnp.max(s, axis=0, keepdims=True)
        e = jnp.exp2(s - m)
        l = jnp.sum(e, axis=0, keepdims=True)
        v_t = jnp.concatenate([vt_ref[0, p, wb + c] for c in range(win // tq)], axis=1)
        o_t = _mm(v_t, e.astype(BF16)) * (1.0 / l)
        o_t = jnp.where(first, o_t[:, :tq], o_t[:, tq:])
        o_ref[0, :, p * HEAD_W:(p + 1) * HEAD_W] = o_t.T.astype(o_ref.dtype)

    s_next = scores(0)
    for pp in range(pairs // 2):
        s_cur = s_next
        if pp + 1 < pairs // 2:
            s_next = scores(pp + 1)
        attend(2 * pp, s_cur[:, :2 * tq])
        attend(2 * pp + 1, s_cur[:, 2 * tq:])


def _attn_a(q_t, k, v_t, bias):
    b, s, _ = k.shape
    tq = ATTN_A_TQ
    win = tq + LEFT
    pairs = ATTN_A_HEADS // 2
    gw = pairs * HEAD_W
    assert bias.shape == (HEADS_A // 2, win + LEFT, 2 * tq)
    return pl.pallas_call(
        functools.partial(_attn_a_kernel, tq=tq, win=win),
        grid=(b, D_MODEL // gw, s // tq),
        in_specs=[
            pl.BlockSpec((1, pairs, 1, HEAD_W, tq), lambda bi, hg, t: (bi, hg, t, 0, 0)),
            pl.BlockSpec((1, s, gw), lambda bi, hg, t: (bi, 0, hg), pipeline_mode=pl.Buffered(1)),
            pl.BlockSpec((1, pairs, s // tq, HEAD_W, tq), lambda bi, hg, t: (bi, hg, 0, 0, 0),
                         pipeline_mode=pl.Buffered(1)),
            pl.BlockSpec((pairs, win + LEFT, 2 * tq), lambda bi, hg, t: (hg, 0, 0),
                         pipeline_mode=pl.Buffered(1)),
        ],
        out_specs=pl.BlockSpec((1, tq, gw), lambda bi, hg, t: (bi, t, hg)),
        out_shape=jax.ShapeDtypeStruct((b, s, D_MODEL), BF16),
        compiler_params=_params(3),
        name="attn_a",
    )(q_t, k, v_t, bias)


def _gmlp_in_kernel(x_ref, g_ref, w_ref, lng_ref, lnb_ref, u_ref, v_ref, *, g_row):
    chains = [slice(r * FFN_SUB, (r + 1) * FFN_SUB) for r in range(x_ref.shape[0] // FFN_SUB)]
    hws = [_mm(_rms(x_ref[rows, :], g_ref[g_row:g_row + 1, :]).astype(BF16), w_ref[...])
           for rows in chains]
    for rows, hw in zip(chains, hws):
        act = 0.5 * hw * (1.0 + lax.erf(hw * math.sqrt(0.5)))
        u_ref[rows, :] = act[:, :D_MODEL].astype(u_ref.dtype)
        v = act[:, D_MODEL:]
        mu = jnp.mean(v, axis=-1, keepdims=True)
        vc = v - mu
        var = jnp.mean(vc * vc, axis=-1, keepdims=True)
        v_ref[rows, :] = (vc * lax.rsqrt(var + EPS) * lng_ref[...] + lnb_ref[...]).astype(v_ref.dtype)


def _gmlp_in(x, g, w, ln_g, ln_b, g_row):
    t, d = x.shape
    out = jax.ShapeDtypeStruct((t, d), BF16)
    row_spec = pl.BlockSpec((ROW_TILE, d), lambda i: (i, 0))
    return pl.pallas_call(
        functools.partial(_gmlp_in_kernel, g_row=g_row),
        grid=(t // ROW_TILE,),
        in_specs=[row_spec, _resident(g), _resident(w),
                  _resident(ln_g), _resident(ln_b)],
        out_specs=[row_spec, row_spec],
        out_shape=[out, out],
        compiler_params=_params(1),
        name="gmlp_in",
    )(x, *_operands(g, w, ln_g, ln_b))


def _gmlp_ffn_kernel(u_ref, v_ref, ws_ref, bs_ref, wo_ref, x_ref, g_ref, win_ref, wout_ref,
                     out_ref, y_ref, x2_ref, *, g_row):
    c = GMLP_CHUNK
    row = lax.broadcasted_iota(jnp.int32, (c, c), 0)
    col = lax.broadcasted_iota(jnp.int32, (c, c), 1)
    causal = row >= col
    for grp in range(GMLP_GROUPS):
        w = ws_ref[grp]
        w = jnp.where(causal, w, jnp.zeros_like(w))
        b = bs_ref[:, grp:grp + 1]
        lanes = slice(grp * c, (grp + 1) * c)
        for blk in range(x_ref.shape[0] // c):
            rows = slice(blk * c, (blk + 1) * c)
            sv = _mm(w, v_ref[rows, lanes]) + b
            y_ref[rows, lanes] = (u_ref[rows, lanes].astype(F32) * sv).astype(y_ref.dtype)
    g = g_ref[...]
    for r in range(x_ref.shape[0] // FFN_SUB):
        rows = slice(r * FFN_SUB, (r + 1) * FFN_SUB)
        m = _mm(y_ref[rows, :], wo_ref[...])
        x2_ref[rows, :] = x_ref[rows, :] + _rms(m, g[g_row - 1:g_row])
    _ffn_chains(x2_ref, g, win_ref, wout_ref, out_ref, g_row)


def _gmlp_ffn(u, v, w_s, b_s_t, w_o, x, g, w_in, w_out, g_row):
    t, d = x.shape
    row_spec = pl.BlockSpec((ROW_TILE, d), lambda i: (i, 0))
    return pl.pallas_call(
        functools.partial(_gmlp_ffn_kernel, g_row=g_row),
        grid=(t // ROW_TILE,),
        in_specs=[row_spec, row_spec, _resident(w_s), _resident(b_s_t), _resident(w_o), row_spec,
                  _resident(g), _resident(w_in), _resident(w_out)],
        out_specs=row_spec,
        out_shape=jax.ShapeDtypeStruct((t, d), F32),
        scratch_shapes=[pltpu.VMEM((ROW_TILE, d), BF16), pltpu.VMEM((ROW_TILE, d), F32)],
        compiler_params=_params(1),
        name="gmlp_ffn",
    )(u, v, *_operands(w_s, b_s_t, w_o), x, *_operands(g, w_in, w_out))


def _attn_c_kernel(lam_ref, subg_ref, qt_ref, k_ref, vt_ref, o_ref,
                   qs_ref, kaug_ref, s_ref, p_ref, diag_ref, smax_ref, m_ref, alpha_ref, acc_ref,
                   *, tile, lambda_init):
    t = pl.program_id(2)
    n = 2 * tile
    lanes = HEAD_W
    group = tile
    n_aug = 2 * len(LOG2_E_BF16)
    ones_rows = jnp.ones((_pad_rows(BF16), tile), BF16)
    first = lax.broadcasted_iota(jnp.int32, (HEAD_W, tile), 0) < HEAD_W // 2
    lam = lam_ref[...]
    lam_full = (jnp.exp(jnp.sum(lam[0:1] * lam[1:2], axis=-1, keepdims=True))
                - jnp.exp(jnp.sum(lam[2:3] * lam[3:4], axis=-1, keepdims=True)) + lambda_init)

    class Head:
        def __init__(self, hh):
            self.hh = hh
            self.cols = slice(hh * HEAD_W, (hh + 1) * HEAD_W)
            h = pl.program_id(1) * ATTN_C_HEADS + hh
            self.inv_slope = jnp.left_shift(1, jnp.full((1, lanes), h + 1, jnp.int32)).astype(F32)

        def build_tables(self):
            hh = self.hh
            key = lax.broadcasted_iota(jnp.int32, (tile, lanes), 0)
            lane = lax.broadcasted_iota(jnp.int32, (tile, lanes), 1)
            lo = (key & 255).astype(F32) / self.inv_slope
            hi = (key & 256).astype(F32) / self.inv_slope
            kaug_ref[hh] = jnp.where(lane < n_aug // 2, lo,
                                     jnp.where(lane < n_aug, hi, 0.0)).astype(BF16)
            row = lax.broadcasted_iota(jnp.int32, (HEAD_W, n), 0)
            consts = jnp.zeros((HEAD_W, n), F32)
            for i, c in enumerate(LOG2_E_BF16 + LOG2_E_BF16):
                consts = jnp.where(row == i, c, consts)
            qs_ref[hh, HEAD_W:, :] = consts.astype(BF16)
            for c in range(tile // lanes):
                qry = lax.broadcasted_iota(jnp.int32, (tile, lanes), 1) + c * lanes
                future = jnp.minimum(2 * (qry - key), 0).astype(F32) / self.inv_slope * LOG2_E
                allowed = jnp.right_shift(key, 6) <= jnp.right_shift(qry, 6)
                diag_ref[hh, c, :tile, :] = jnp.where(allowed, future, NEG_INF)

        def start_tile(self):
            hh = self.hh
            qt = qt_ref[0, hh, 0]
            zero = jnp.zeros_like(qt)
            qs_ref[hh, :HEAD_W, :tile] = jnp.where(first, qt, zero)
            qs_ref[hh, :HEAD_W, tile:] = jnp.where(first, zero, qt)
            m_ref[hh] = jnp.full((1, n), NEG_INF, F32)
            acc_ref[hh] = jnp.zeros(acc_ref.shape[1:], F32)
            p_ref[hh, 1] = jnp.zeros(p_ref.shape[2:], BF16)
            alpha_ref[hh, 1] = jnp.ones((1, n), F32)

        def issue_scores(self, j, slot):
            hh = self.hh
            start = pl.multiple_of(j * tile, tile)
            keys = jnp.concatenate([k_ref[0, pl.ds(start, tile), self.cols], kaug_ref[hh]], axis=1)
            for g in range(n // group):
                gcols = slice(g * group, (g + 1) * group)
                s = _mm(keys, qs_ref[hh, :, gcols])
                for c in range(group // lanes):
                    s_ref[hh, slot, g * (group // lanes) + c, :tile, :] = (
                        s[:, c * lanes:(c + 1) * lanes])
                smax_ref[hh, slot, :, gcols] = jnp.max(s, axis=0, keepdims=True)

        def softmax(self, i, slot, own_block):
            hh = self.hh
            if own_block:
                shift = jnp.zeros((1, lanes), F32)
            else:
                shift = (jnp.full((1, lanes), (i - t) * tile, jnp.int32).astype(F32)
                         / self.inv_slope * sum(LOG2_E_BF16))
            for c in range(n // lanes):
                cols = slice(c * lanes, (c + 1) * lanes)
                if own_block:
                    q_tile = c % (tile // lanes)
                    keys = (q_tile + 1) * lanes
                    sc = s_ref[hh, slot, c, :keys, :] + diag_ref[hh, q_tile, :keys, :]
                    s_ref[hh, slot, c, :keys, :] = sc
                    smax = jnp.max(sc, axis=0, keepdims=True)
                    if keys < tile:
                        p_ref[hh, slot, c, keys:tile, :] = jnp.zeros((tile - keys, lanes), BF16)
                else:
                    keys = tile
                    smax = smax_ref[hh, slot, :, cols]
                m_old = m_ref[hh, :, cols]
                m_new = jnp.maximum(m_old, smax + shift)
                alpha = jnp.exp2(m_old - m_new)
                e = jnp.exp2(s_ref[hh, slot, c, :keys, :] - (m_new - shift))
                m_ref[hh, :, cols] = m_new
                p_ref[hh, slot, c, :keys, :] = e.astype(BF16)
                alpha_ref[hh, slot, :, cols] = alpha

        def values(self, j, slot):
            hh = self.hh
            p = jnp.concatenate([p_ref[hh, slot, c, :tile, :] for c in range(n // lanes)], axis=1)
            v_t = jnp.concatenate([vt_ref[0, hh, j], ones_rows], axis=0)
            acc_ref[hh] = alpha_ref[hh, slot] * acc_ref[hh] + _mm(v_t, p)

        def finish_tile(self):
            hh = self.hh
            o_t = acc_ref[hh, :HEAD_W, :] * (1.0 / acc_ref[hh, HEAD_W:HEAD_W + 1, :])
            a = (o_t[:, :tile] - lam_full * o_t[:, tile:]).T
            o_ref[0, :, self.cols] = (_rms(a, subg_ref[...])
                                      * (1.0 - lambda_init)).astype(o_ref.dtype)

    heads = [Head(hh) for hh in range(ATTN_C_HEADS)]

    @pl.when(t == 0)
    def _():
        for head in heads:
            head.build_tables()

    def step(i, slot, own_block):
        for head in heads:
            head.values(jnp.maximum(i - 1, 0), 1 - slot)
        if not own_block:
            for head in heads:
                head.issue_scores(i + 1, 1 - slot)
        for head in heads:
            head.softmax(i, slot, own_block)

    for head in heads:
        head.start_tile()
    for head in heads:
        head.issue_scores(0, 0)

    def past_blocks(start, count):
        for k in range(count):
            step(start + k, k % 2, False)

    def unrolled_past_blocks(u, carry):
        past_blocks(ATTN_C_UNROLL * u, ATTN_C_UNROLL)
        return carry

    lax.fori_loop(0, t // ATTN_C_UNROLL, unrolled_past_blocks, 0)

    for left in range(ATTN_C_UNROLL):
        @pl.when(t % ATTN_C_UNROLL == left)
        def _(left=left):
            past_blocks(t - left, left)
            step(t, left % 2, True)
            for head in heads:
                head.values(t, left % 2)

    for head in heads:
        head.finish_tile()


def _attn_c(q_t, k, v_t, lam, subln_g, lambda_init):
    b, s, _ = k.shape
    tile = ATTN_C_TILE
    n = 2 * tile
    hp = ATTN_C_HEADS
    return pl.pallas_call(
        functools.partial(_attn_c_kernel, tile=tile, lambda_init=lambda_init),
        grid=(b, HEADS_C // hp, s // tile),
        in_specs=[
            _resident(lam),
            _resident(subln_g),
            pl.BlockSpec((1, hp, 1, HEAD_W, tile), lambda bi, h, t: (bi, h, t, 0, 0)),
            pl.BlockSpec((1, s, hp * HEAD_W), lambda bi, h, t: (bi, 0, h)),
            pl.BlockSpec((1, hp, s // tile, HEAD_W, tile), lambda bi, h, t: (bi, h, 0, 0, 0)),
        ],
        out_specs=pl.BlockSpec((1, tile, hp * HEAD_W), lambda bi, h, t: (bi, t, h)),
        out_shape=jax.ShapeDtypeStruct((b, s, D_MODEL), BF16),
        scratch_shapes=[
            pltpu.VMEM((hp, 2 * HEAD_W, n), BF16),
            pltpu.VMEM((hp, tile, HEAD_W), BF16),
            pltpu.VMEM((hp, 2, n // HEAD_W, tile + _pad_rows(F32), HEAD_W), F32),
            pltpu.VMEM((hp, 2, n // HEAD_W, tile + _pad_rows(BF16), HEAD_W), BF16),
            pltpu.VMEM((hp, tile // HEAD_W, tile + _pad_rows(F32), HEAD_W), F32),
            pltpu.VMEM((hp, 2, 1, n), F32),
            pltpu.VMEM((hp, 1, n), F32),
            pltpu.VMEM((hp, 2, 1, n), F32),
            pltpu.VMEM((hp, HEAD_W + _pad_rows(BF16), n), F32),
        ],
        compiler_params=_params(3),
        name="attn_c",
    )(*_operands(lam, subln_g), q_t, k, v_t)


def kernel(x, norm_g, ff1_w_in, ff1_w_out, ff2_w_in, ff2_w_out, a_w_qkv, a_rel_bias, a_w_o,
           b_w_in, b_ln_g, b_ln_b, b_w_s, b_b_s, b_w_o, c_w_qkv, c_lambda, c_subln_g, c_w_o):
    b, s, d = x.shape
    depth = norm_g.shape[0]
    ff1_w_in, ff1_w_out, ff2_w_in, ff2_w_out, a_w_o, b_w_in, b_w_s, b_w_o, c_w_o = (
        w.astype(BF16) for w in
        (ff1_w_in, ff1_w_out, ff2_w_in, ff2_w_out, a_w_o, b_w_in, b_w_s, b_w_o, c_w_o))

    def split_qkv(w):
        w = w.astype(BF16)
        return w[:, d:2 * d], jnp.concatenate([w[:, :d], w[:, 2 * d:]], axis=1).T

    b_ln_g, b_ln_b, c_subln_g = b_ln_g[:, None], b_ln_b[:, None], c_subln_g[:, None]
    b_b_s_t = jnp.swapaxes(b_b_s, 1, 2)
    xf = x.reshape(b * s, d)
    for i in range(depth):
        g = _Layer(norm_g, i)
        xf = _ffn(xf, g, _Layer(ff1_w_in, i), _Layer(ff1_w_out, i), 0)
        ff2 = (_Layer(ff2_w_in, i), _Layer(ff2_w_out, i), 4)
        kind, j = i % N_MIXERS, i // N_MIXERS
        if kind == 0:
            w_k, w_qv_t = split_qkv(a_w_qkv[j])
            q_t, k, v_t = _norm_proj_t(xf, g, w_k, w_qv_t, 2, b, ATTN_A_TQ, ATTN_A_TQ)
            bias = _attn_a_bias(a_rel_bias[j], ATTN_A_TQ, ATTN_A_TQ + LEFT)
            o = _attn_a(q_t, k.reshape(b, s, d), v_t, bias)
            xf = _proj_ffn(o.reshape(b * s, d), _Layer(a_w_o, j), xf, g, *ff2)
        elif kind == 1:
            u, v = _gmlp_in(xf, g, _Layer(b_w_in, j), _Layer(b_ln_g, j), _Layer(b_ln_b, j), 2)
            xf = _gmlp_ffn(u, v, _Layer(b_w_s, j), _Layer(b_b_s_t, j), _Layer(b_w_o, j), xf, g, *ff2)
        else:
            lambda_init = 0.8 - 0.6 * math.exp(-0.3 * i)
            w_k, w_qv_t = split_qkv(c_w_qkv[j])
            q_t, k, v_t = _norm_proj_t(xf, g, w_k, w_qv_t, 2, b, ATTN_C_TILE, ATTN_C_TILE)
            o = _attn_c(q_t, k.reshape(b, s, d), v_t, _Layer(c_lambda, j), _Layer(c_subln_g, j),
                        lambda_init)
            xf = _proj_ffn(o.reshape(b * s, d), _Layer(c_w_o, j), xf, g, *ff2)
    return xf.reshape(b, s, d)
```

```python
import functools
import math
from typing import NamedTuple

import jax
import jax.numpy as jnp
import numpy as np
from jax import lax
from jax.experimental import pallas as pl
from jax.experimental.pallas import tpu as pltpu

F32 = jnp.float32
BF16 = jnp.bfloat16

D_MODEL = 1024
D_FF = 2816
EPS = 1e-6
NEG_INF = -1e30
N_MIXERS = 3

CHUNK = 64
LEFT = 8 * CHUNK
HEADS_A = 16
REL_CLIP = 128
GMLP_CHUNK = 128
GMLP_GROUPS = 8
HEADS_C = 8
HEAD_W = 128
LOG2_E = math.log2(math.e)
LOG2_E_BF16 = (1.4453125, -0.00262451171875, 7.063150405883789e-06)

V7X_VMEM_LIMIT_BYTES = 56 * 1024 * 1024
V7X_VREG_BYTES = 8 * 128 * 4


def _pad_rows(dtype):
    return V7X_VREG_BYTES // (128 * jnp.dtype(dtype).itemsize)

ROW_TILE = 512
QKV_ROW_TILE = 1024
FFN_ROW_TILE = 1024
PROJ_FFN_ROW_TILE = 1024
FFN_SUB = 256
ATTN_A_TQ = 128
ATTN_A_HEADS = 16
ATTN_C_TILE = 512
ATTN_C_HEADS = 4
ATTN_C_UNROLL = 2


def _params(n_axes):
    return pltpu.CompilerParams(
        dimension_semantics=("arbitrary",) * n_axes,
        vmem_limit_bytes=V7X_VMEM_LIMIT_BYTES,
    )


class _Layer(NamedTuple):
    stack: jax.Array
    index: int


def _resident(p):
    if isinstance(p, _Layer):
        tail = p.stack.shape[1:]
        return pl.BlockSpec((None,) + tail, lambda *_: (p.index,) + (0,) * len(tail),
                            pipeline_mode=pl.Buffered(1))
    return pl.BlockSpec(p.shape, lambda *_: (0,) * p.ndim, pipeline_mode=pl.Buffered(1))


def _operands(*params):
    return [p.stack if isinstance(p, _Layer) else p for p in params]


def _rms(x, g):
    return x * lax.rsqrt(jnp.mean(x * x, axis=-1, keepdims=True) + EPS) * g


def _mm(a, b):
    return jnp.dot(a, b, preferred_element_type=F32)


def _mm_nt(a, b):
    return lax.dot_general(a, b, (((1,), (1,)), ((), ())), preferred_element_type=F32)


def _ffn_chains(x_ref, g, win_ref, wout_ref, o_ref, g_row):
    chains = [slice(r * FFN_SUB, (r + 1) * FFN_SUB) for r in range(x_ref.shape[0] // FFN_SUB)]

    def up_proj(rows):
        return _mm(_rms(x_ref[rows, :], g[g_row:g_row + 1]).astype(BF16), win_ref[...])

    def down_proj(gu):
        gate = gu[:, :D_FF]
        up = gu[:, D_FF:]
        h = (gate * (1.0 / (1.0 + jnp.exp(-gate))) * up).astype(BF16)
        return _mm(h, wout_ref[...])

    gu_next = up_proj(chains[0])
    y_prev = None
    for r in range(len(chains)):
        gu = gu_next
        if r + 1 < len(chains):
            gu_next = up_proj(chains[r + 1])
        y = down_proj(gu)
        if y_prev is not None:
            o_ref[chains[r - 1], :] = (x_ref[chains[r - 1], :]
                                       + 0.5 * _rms(y_prev, g[g_row + 1:g_row + 2]))
        y_prev = y
    o_ref[chains[-1], :] = x_ref[chains[-1], :] + 0.5 * _rms(y_prev, g[g_row + 1:g_row + 2])


def _ffn_kernel(x_ref, g_ref, win_ref, wout_ref, o_ref, *, g_row):
    _ffn_chains(x_ref, g_ref[...], win_ref, wout_ref, o_ref, g_row)


def _proj_ffn_kernel(a_ref, wo_ref, x_ref, g_ref, win_ref, wout_ref, o_ref, x2_ref, *, g_row):
    g = g_ref[...]
    for r in range(x_ref.shape[0] // FFN_SUB):
        rows = slice(r * FFN_SUB, (r + 1) * FFN_SUB)
        m = _mm(a_ref[rows, :], wo_ref[...])
        x2_ref[rows, :] = x_ref[rows, :] + _rms(m, g[g_row - 1:g_row])
    _ffn_chains(x2_ref, g, win_ref, wout_ref, o_ref, g_row)


def _proj_ffn(a, w_o, x, g, w_in, w_out, g_row):
    t, d = x.shape
    rows = pl.BlockSpec((PROJ_FFN_ROW_TILE, d), lambda i: (i, 0))
    return pl.pallas_call(
        functools.partial(_proj_ffn_kernel, g_row=g_row),
        grid=(t // PROJ_FFN_ROW_TILE,),
        in_specs=[rows, _resident(w_o), rows, _resident(g), _resident(w_in), _resident(w_out)],
        out_specs=rows,
        out_shape=jax.ShapeDtypeStruct((t, d), F32),
        scratch_shapes=[pltpu.VMEM((PROJ_FFN_ROW_TILE, d), F32)],
        compiler_params=_params(1),
        name="proj_ffn",
    )(a, *_operands(w_o), x, *_operands(g, w_in, w_out))


def _ffn(x, g, w_in, w_out, g_row):
    t, d = x.shape
    return pl.pallas_call(
        functools.partial(_ffn_kernel, g_row=g_row),
        grid=(t // FFN_ROW_TILE,),
        in_specs=[
            pl.BlockSpec((FFN_ROW_TILE, d), lambda i: (i, 0)),
            _resident(g),
            _resident(w_in),
            _resident(w_out),
        ],
        out_specs=pl.BlockSpec((FFN_ROW_TILE, d), lambda i: (i, 0)),
        out_shape=jax.ShapeDtypeStruct((t, d), F32),
        compiler_params=_params(1),
        name="ffn",
    )(x, *_operands(g, w_in, w_out))


N_GROUPS = D_MODEL // HEAD_W


def _norm_proj_t_kernel(x_ref, g_ref, wk_ref, wqv_ref, qt_ref, k_ref, vt_ref, *, g_row, tq, tv):
    chain = ROW_TILE
    chains = [slice(r * chain, (r + 1) * chain) for r in range(x_ref.shape[0] // chain)]
    xns = [_rms(x_ref[rows, :], g_ref[g_row:g_row + 1, :]).astype(BF16) for rows in chains]
    for r, (rows, xn) in enumerate(zip(chains, xns)):
        k_ref[rows, :] = _mm(xn, wk_ref[...]).astype(k_ref.dtype)
        qv_t = _mm_nt(wqv_ref[...], xn)
        q_t = (qv_t[:D_MODEL] * ((HEAD_W // 2) ** -0.5 * LOG2_E)).astype(BF16)
        v_t = qv_t[D_MODEL:].astype(BF16)
        for part in range(chain // tq):
            tile = q_t[:, part * tq:(part + 1) * tq].reshape(N_GROUPS, HEAD_W, tq)
            qt_ref[0, :, r * (chain // tq) + part] = tile
        for part in range(chain // tv):
            tile = v_t[:, part * tv:(part + 1) * tv].reshape(N_GROUPS, HEAD_W, tv)
            vt_ref[0, :, r * (chain // tv) + part] = tile


def _norm_proj_t(x, g, w_k, w_qv_t, g_row, batch, tq, tv):
    t, d = x.shape
    s = t // batch
    rows = QKV_ROW_TILE
    per_batch = s // rows

    def tiles(width):
        return pl.BlockSpec((1, N_GROUPS, rows // width, HEAD_W, width),
                            lambda i: (i // per_batch, 0, i % per_batch, 0, 0))

    return pl.pallas_call(
        functools.partial(_norm_proj_t_kernel, g_row=g_row, tq=tq, tv=tv),
        grid=(t // rows,),
        in_specs=[
            pl.BlockSpec((rows, d), lambda i: (i, 0)),
            _resident(g),
            _resident(w_k),
            _resident(w_qv_t),
        ],
        out_specs=[tiles(tq), pl.BlockSpec((rows, d), lambda i: (i, 0)), tiles(tv)],
        out_shape=[
            jax.ShapeDtypeStruct((batch, N_GROUPS, s // tq, HEAD_W, tq), BF16),
            jax.ShapeDtypeStruct((t, d), BF16),
            jax.ShapeDtypeStruct((batch, N_GROUPS, s // tv, HEAD_W, tv), BF16),
        ],
        compiler_params=_params(1),
        name="norm_proj_t",
    )(x, *_operands(g, w_k, w_qv_t))


def _attn_a_bias(rel_bias, tq, win):
    heads = rel_bias.shape[0]
    tbl = rel_bias.astype(F32)
    f = jnp.concatenate([
        jnp.broadcast_to(tbl[:, 2 * REL_CLIP:], (heads, tq + LEFT - REL_CLIP)),
        tbl[:, 2 * REL_CLIP - 1:0:-1],
    ], axis=1)
    p = win + tq
    skew = jnp.tile(jnp.pad(f, ((0, 0), (0, 1))), (1, tq))[:, :tq * (p - 1)]
    bias = skew.reshape(heads, tq, p - 1)[:, :, tq - 1:tq - 1 + win]
    r = np.arange(tq)[:, None] // CHUNK
    j = np.arange(win)[None, :]
    valid = (j >= r * CHUNK) & (j < (r + LEFT // CHUNK + 1) * CHUNK)
    bias = jnp.where(valid, bias * LOG2_E, NEG_INF)
    bias = jnp.pad(bias, ((0, 0), (0, 0), (0, LEFT)), constant_values=NEG_INF)
    bias = bias.reshape(HEADS_A // 2, 2, tq, win + LEFT)
    return jnp.transpose(bias, (0, 3, 1, 2)).reshape(HEADS_A // 2, win + LEFT, 2 * tq)


def _attn_a_kernel(qt_ref, k_ref, vt_ref, bias_ref, o_ref, *, tq, win):
    t = pl.program_id(2)
    wb = jnp.maximum(t - LEFT // tq, 0)
    ws = pl.multiple_of(wb * tq, tq)
    shift = pl.multiple_of(jnp.maximum(LEFT - t * tq, 0), tq)
    first = lax.broadcasted_iota(jnp.int32, (HEAD_W, tq), 0) < HEAD_W // 2
    pairs = ATTN_A_HEADS // 2

    def scores(pp):
        zero = jnp.zeros((HEAD_W, tq), BF16)
        blocks = []
        for p in (2 * pp, 2 * pp + 1):
            qt = qt_ref[0, p, 0]
            row = [zero] * 4
            row[2 * (p % 2)] = jnp.where(first, qt, zero)
            row[2 * (p % 2) + 1] = jnp.where(first, zero, qt)
            blocks.append(jnp.concatenate(row, axis=1))
        qs = jnp.concatenate(blocks, axis=0)
        keys = k_ref[0, pl.ds(ws, win), 2 * pp * HEAD_W:(2 * pp + 2) * HEAD_W]
        bias = jnp.concatenate([bias_ref[2 * pp, pl.ds(shift, win), :],
                                bias_ref[2 * pp + 1, pl.ds(shift, win), :]], axis=1)
        return _mm(keys, qs) + bias

    def attend(p, s):
        m = jnp.max(s, axis=0, keepdims=True)
        e = jnp.exp2(s - m)
        l = jnp.sum(e, axis=0, keepdims=True)
        v_t = jnp.concatenate([vt_ref[0, p, wb + c] for c in range(win // tq)], axis=1)
        o_t = _mm(v_t, e.astype(BF16)) * (1.0 / l)
        o_t = jnp.where(first, o_t[:, :tq], o_t[:, tq:])
        o_ref[0, :, p * HEAD_W:(p + 1) * HEAD_W] = o_t.T.astype(o_ref.dtype)

    s_next = scores(0)
    for pp in range(pairs // 2):
        s_cur = s_next
        if pp + 1 < pairs // 2:
            s_next = scores(pp + 1)
        attend(2 * pp, s_cur[:, :2 * tq])
        attend(2 * pp + 1, s_cur[:, 2 * tq:])


def _attn_a(q_t, k, v_t, bias):
    b, s, _ = k.shape
    tq = ATTN_A_TQ
    win = tq + LEFT
    pairs = ATTN_A_HEADS // 2
    gw = pairs * HEAD_W
    assert bias.shape == (HEADS_A // 2, win + LEFT, 2 * tq)
    return pl.pallas_call(
        functools.partial(_attn_a_kernel, tq=tq, win=win),
        grid=(b, D_MODEL // gw, s // tq),
        in_specs=[
            pl.BlockSpec((1, pairs, 1, HEAD_W, tq), lambda bi, hg, t: (bi, hg, t, 0, 0)),
            pl.BlockSpec((1, s, gw), lambda bi, hg, t: (bi, 0, hg), pipeline_mode=pl.Buffered(1)),
            pl.BlockSpec((1, pairs, s // tq, HEAD_W, tq), lambda bi, hg, t: (bi, hg, 0, 0, 0),
                         pipeline_mode=pl.Buffered(1)),
            pl.BlockSpec((pairs, win + LEFT, 2 * tq), lambda bi, hg, t: (hg, 0, 0),
                         pipeline_mode=pl.Buffered(1)),
        ],
        out_specs=pl.BlockSpec((1, tq, gw), lambda bi, hg, t: (bi, t, hg)),
        out_shape=jax.ShapeDtypeStruct((b, s, D_MODEL), BF16),
        compiler_params=_params(3),
        name="attn_a",
    )(q_t, k, v_t, bias)


def _gmlp_in_kernel(x_ref, g_ref, w_ref, lng_ref, lnb_ref, u_ref, v_ref, *, g_row):
    chains = [slice(r * FFN_SUB, (r + 1) * FFN_SUB) for r in range(x_ref.shape[0] // FFN_SUB)]
    hws = [_mm(_rms(x_ref[rows, :], g_ref[g_row:g_row + 1, :]).astype(BF16), w_ref[...])
           for rows in chains]
    for rows, hw in zip(chains, hws):
        act = 0.5 * hw * (1.0 + lax.erf(hw * math.sqrt(0.5)))
        u_ref[rows, :] = act[:, :D_MODEL].astype(u_ref.dtype)
        v = act[:, D_MODEL:]
        mu = jnp.mean(v, axis=-1, keepdims=True)
        vc = v - mu
        var = jnp.mean(vc * vc, axis=-1, keepdims=True)
        v_ref[rows, :] = (vc * lax.rsqrt(var + EPS) * lng_ref[...] + lnb_ref[...]).astype(v_ref.dtype)


def _gmlp_in(x, g, w, ln_g, ln_b, g_row):
    t, d = x.shape
    out = jax.ShapeDtypeStruct((t, d), BF16)
    row_spec = pl.BlockSpec((ROW_TILE, d), lambda i: (i, 0))
    return pl.pallas_call(
        functools.partial(_gmlp_in_kernel, g_row=g_row),
        grid=(t // ROW_TILE,),
        in_specs=[row_spec, _resident(g), _resident(w),
                  _resident(ln_g), _resident(ln_b)],
        out_specs=[row_spec, row_spec],
        out_shape=[out, out],
        compiler_params=_params(1),
        name="gmlp_in",
    )(x, *_operands(g, w, ln_g, ln_b))


def _gmlp_ffn_kernel(u_ref, v_ref, ws_ref, bs_ref, wo_ref, x_ref, g_ref, win_ref, wout_ref,
                     out_ref, y_ref, x2_ref, *, g_row):
    c = GMLP_CHUNK
    row = lax.broadcasted_iota(jnp.int32, (c, c), 0)
    col = lax.broadcasted_iota(jnp.int32, (c, c), 1)
    causal = row >= col
    for grp in range(GMLP_GROUPS):
        w = ws_ref[grp]
        w = jnp.where(causal, w, jnp.zeros_like(w))
        b = bs_ref[:, grp:grp + 1]
        lanes = slice(grp * c, (grp + 1) * c)
        for blk in range(x_ref.shape[0] // c):
            rows = slice(blk * c, (blk + 1) * c)
            sv = _mm(w, v_ref[rows, lanes]) + b
            y_ref[rows, lanes] = (u_ref[rows, lanes].astype(F32) * sv).astype(y_ref.dtype)
    g = g_ref[...]
    for r in range(x_ref.shape[0] // FFN_SUB):
        rows = slice(r * FFN_SUB, (r + 1) * FFN_SUB)
        m = _mm(y_ref[rows, :], wo_ref[...])
        x2_ref[rows, :] = x_ref[rows, :] + _rms(m, g[g_row - 1:g_row])
    _ffn_chains(x2_ref, g, win_ref, wout_ref, out_ref, g_row)


def _gmlp_ffn(u, v, w_s, b_s_t, w_o, x, g, w_in, w_out, g_row):
    t, d = x.shape
    row_spec = pl.BlockSpec((ROW_TILE, d), lambda i: (i, 0))
    return pl.pallas_call(
        functools.partial(_gmlp_ffn_kernel, g_row=g_row),
        grid=(t // ROW_TILE,),
        in_specs=[row_spec, row_spec, _resident(w_s), _resident(b_s_t), _resident(w_o), row_spec,
                  _resident(g), _resident(w_in), _resident(w_out)],
        out_specs=row_spec,
        out_shape=jax.ShapeDtypeStruct((t, d), F32),
        scratch_shapes=[pltpu.VMEM((ROW_TILE, d), BF16), pltpu.VMEM((ROW_TILE, d), F32)],
        compiler_params=_params(1),
        name="gmlp_ffn",
    )(u, v, *_operands(w_s, b_s_t, w_o), x, *_operands(g, w_in, w_out))


def _attn_c_kernel(lam_ref, subg_ref, qt_ref, k_ref, vt_ref, o_ref,
                   qs_ref, kaug_ref, s_ref, p_ref, diag_ref, smax_ref, m_ref, alpha_ref, acc_ref,
                   *, tile, lambda_init):
    t = pl.program_id(2)
    n = 2 * tile
    lanes = HEAD_W
    group = tile
    n_aug = 2 * len(LOG2_E_BF16)
    ones_rows = jnp.ones((_pad_rows(BF16), tile), BF16)
    first = lax.broadcasted_iota(jnp.int32, (HEAD_W, tile), 0) < HEAD_W // 2
    lam = lam_ref[...]
    lam_full = (jnp.exp(jnp.sum(lam[0:1] * lam[1:2], axis=-1, keepdims=True))
                - jnp.exp(jnp.sum(lam[2:3] * lam[3:4], axis=-1, keepdims=True)) + lambda_init)

    class Head:
        def __init__(self, hh):
            self.hh = hh
            self.cols = slice(hh * HEAD_W, (hh + 1) * HEAD_W)
            h = pl.program_id(1) * ATTN_C_HEADS + hh
            self.inv_slope = jnp.left_shift(1, jnp.full((1, lanes), h + 1, jnp.int32)).astype(F32)

        def build_tables(self):
            hh = self.hh
            key = lax.broadcasted_iota(jnp.int32, (tile, lanes), 0)
            lane = lax.broadcasted_iota(jnp.int32, (tile, lanes), 1)
            lo = (key & 255).astype(F32) / self.inv_slope
            hi = (key & 256).astype(F32) / self.inv_slope
            kaug_ref[hh] = jnp.where(lane < n_aug // 2, lo,
                                     jnp.where(lane < n_aug, hi, 0.0)).astype(BF16)
            row = lax.broadcasted_iota(jnp.int32, (HEAD_W, n), 0)
            consts = jnp.zeros((HEAD_W, n), F32)
            for i, c in enumerate(LOG2_E_BF16 + LOG2_E_BF16):
                consts = jnp.where(row == i, c, consts)
            qs_ref[hh, HEAD_W:, :] = consts.astype(BF16)
            for c in range(tile // lanes):
                qry = lax.broadcasted_iota(jnp.int32, (tile, lanes), 1) + c * lanes
                future = jnp.minimum(2 * (qry - key), 0).astype(F32) / self.inv_slope * LOG2_E
                allowed = jnp.right_shift(key, 6) <= jnp.right_shift(qry, 6)
                diag_ref[hh, c, :tile, :] = jnp.where(allowed, future, NEG_INF)

        def start_tile(self):
            hh = self.hh
            qt = qt_ref[0, hh, 0]
            zero = jnp.zeros_like(qt)
            qs_ref[hh, :HEAD_W, :tile] = jnp.where(first, qt, zero)
            qs_ref[hh, :HEAD_W, tile:] = jnp.where(first, zero, qt)
            m_ref[hh] = jnp.full((1, n), NEG_INF, F32)
            acc_ref[hh] = jnp.zeros(acc_ref.shape[1:], F32)
            p_ref[hh, 1] = jnp.zeros(p_ref.shape[2:], BF16)
            alpha_ref[hh, 1] = jnp.ones((1, n), F32)

        def issue_scores(self, j, slot):
            hh = self.hh
            start = pl.multiple_of(j * tile, tile)
            keys = jnp.concatenate([k_ref[0, pl.ds(start, tile), self.cols], kaug_ref[hh]], axis=1)
            for g in range(n // group):
                gcols = slice(g * group, (g + 1) * group)
                s = _mm(keys, qs_ref[hh, :, gcols])
                for c in range(group // lanes):
                    s_ref[hh, slot, g * (group // lanes) + c, :tile, :] = (
                        s[:, c * lanes:(c + 1) * lanes])
                smax_ref[hh, slot, :, gcols] = jnp.max(s, axis=0, keepdims=True)

        def softmax(self, i, slot, own_block):
            hh = self.hh
            if own_block:
                shift = jnp.zeros((1, lanes), F32)
            else:
                shift = (jnp.full((1, lanes), (i - t) * tile, jnp.int32).astype(F32)
                         / self.inv_slope * sum(LOG2_E_BF16))
            for c in range(n // lanes):
                cols = slice(c * lanes, (c + 1) * lanes)
                if own_block:
                    q_tile = c % (tile // lanes)
                    keys = (q_tile + 1) * lanes
                    sc = s_ref[hh, slot, c, :keys, :] + diag_ref[hh, q_tile, :keys, :]
                    s_ref[hh, slot, c, :keys, :] = sc
                    smax = jnp.max(sc, axis=0, keepdims=True)
                    if keys < tile:
                        p_ref[hh, slot, c, keys:tile, :] = jnp.zeros((tile - keys, lanes), BF16)
                else:
                    keys = tile
                    smax = smax_ref[hh, slot, :, cols]
                m_old = m_ref[hh, :, cols]
                m_new = jnp.maximum(m_old, smax + shift)
                alpha = jnp.exp2(m_old - m_new)
                e = jnp.exp2(s_ref[hh, slot, c, :keys, :] - (m_new - shift))
                m_ref[hh, :, cols] = m_new
                p_ref[hh, slot, c, :keys, :] = e.astype(BF16)
                alpha_ref[hh, slot, :, cols] = alpha

        def values(self, j, slot):
            hh = self.hh
            p = jnp.concatenate([p_ref[hh, slot, c, :tile, :] for c in range(n // lanes)], axis=1)
            v_t = jnp.concatenate([vt_ref[0, hh, j], ones_rows], axis=0)
            acc_ref[hh] = alpha_ref[hh, slot] * acc_ref[hh] + _mm(v_t, p)

        def finish_tile(self):
            hh = self.hh
            o_t = acc_ref[hh, :HEAD_W, :] * (1.0 / acc_ref[hh, HEAD_W:HEAD_W + 1, :])
            a = (o_t[:, :tile] - lam_full * o_t[:, tile:]).T
            o_ref[0, :, self.cols] = (_rms(a, subg_ref[...])
                                      * (1.0 - lambda_init)).astype(o_ref.dtype)

    heads = [Head(hh) for hh in range(ATTN_C_HEADS)]

    @pl.when(t == 0)
    def _():
        for head in heads:
            head.build_tables()

    def step(i, slot, own_block):
        for head in heads:
            head.values(jnp.maximum(i - 1, 0), 1 - slot)
        if not own_block:
            for head in heads:
                head.issue_scores(i + 1, 1 - slot)
        for head in heads:
            head.softmax(i, slot, own_block)

    for head in heads:
        head.start_tile()
    for head in heads:
        head.issue_scores(0, 0)

    def past_blocks(start, count):
        for k in range(count):
            step(start + k, k % 2, False)

    def unrolled_past_blocks(u, carry):
        past_blocks(ATTN_C_UNROLL * u, ATTN_C_UNROLL)
        return carry

    lax.fori_loop(0, t // ATTN_C_UNROLL, unrolled_past_blocks, 0)

    for left in range(ATTN_C_UNROLL):
        @pl.when(t % ATTN_C_UNROLL == left)
        def _(left=left):
            past_blocks(t - left, left)
            step(t, left % 2, True)
            for head in heads:
                head.values(t, left % 2)

    for head in heads:
        head.finish_tile()


def _attn_c(q_t, k, v_t, lam, subln_g, lambda_init):
    b, s, _ = k.shape
    tile = ATTN_C_TILE
    n = 2 * tile
    hp = ATTN_C_HEADS
    return pl.pallas_call(
        functools.partial(_attn_c_kernel, tile=tile, lambda_init=lambda_init),
        grid=(b, HEADS_C // hp, s // tile),
        in_specs=[
            _resident(lam),
            _resident(subln_g),
            pl.BlockSpec((1, hp, 1, HEAD_W, tile), lambda bi, h, t: (bi, h, t, 0, 0)),
            pl.BlockSpec((1, s, hp * HEAD_W), lambda bi, h, t: (bi, 0, h),
                         pipeline_mode=pl.Buffered(1)),
            pl.BlockSpec((1, hp, s // tile, HEAD_W, tile), lambda bi, h, t: (bi, h, 0, 0, 0),
                         pipeline_mode=pl.Buffered(1)),
        ],
        out_specs=pl.BlockSpec((1, tile, hp * HEAD_W), lambda bi, h, t: (bi, t, h)),
        out_shape=jax.ShapeDtypeStruct((b, s, D_MODEL), BF16),
        scratch_shapes=[
            pltpu.VMEM((hp, 2 * HEAD_W, n), BF16),
            pltpu.VMEM((hp, tile, HEAD_W), BF16),
            pltpu.VMEM((hp, 2, n // HEAD_W, tile + _pad_rows(F32), HEAD_W), F32),
            pltpu.VMEM((hp, 2, n // HEAD_W, tile + _pad_rows(BF16), HEAD_W), BF16),
            pltpu.VMEM((hp, tile // HEAD_W, tile + _pad_rows(F32), HEAD_W), F32),
            pltpu.VMEM((hp, 2, 1, n), F32),
            pltpu.VMEM((hp, 1, n), F32),
            pltpu.VMEM((hp, 2, 1, n), F32),
            pltpu.VMEM((hp, HEAD_W + _pad_rows(BF16), n), F32),
        ],
        compiler_params=_params(3),
        name="attn_c",
    )(*_operands(lam, subln_g), q_t, k, v_t)


def kernel(x, norm_g, ff1_w_in, ff1_w_out, ff2_w_in, ff2_w_out, a_w_qkv, a_rel_bias, a_w_o,
           b_w_in, b_ln_g, b_ln_b, b_w_s, b_b_s, b_w_o, c_w_qkv, c_lambda, c_subln_g, c_w_o):
    b, s, d = x.shape
    depth = norm_g.shape[0]
    ff1_w_in, ff1_w_out, ff2_w_in, ff2_w_out, a_w_o, b_w_in, b_w_s, b_w_o, c_w_o = (
        w.astype(BF16) for w in
        (ff1_w_in, ff1_w_out, ff2_w_in, ff2_w_out, a_w_o, b_w_in, b_w_s, b_w_o, c_w_o))

    def split_qkv(w):
        w = w.astype(BF16)
        return w[:, d:2 * d], jnp.concatenate([w[:, :d], w[:, 2 * d:]], axis=1).T

    b_ln_g, b_ln_b, c_subln_g = b_ln_g[:, None], b_ln_b[:, None], c_subln_g[:, None]
    b_b_s_t = jnp.swapaxes(b_b_s, 1, 2)
    xf = x.reshape(b * s, d)
    for i in range(depth):
        g = _Layer(norm_g, i)
        xf = _ffn(xf, g, _Layer(ff1_w_in, i), _Layer(ff1_w_out, i), 0)
        ff2 = (_Layer(ff2_w_in, i), _Layer(ff2_w_out, i), 4)
        kind, j = i % N_MIXERS, i // N_MIXERS
        if kind == 0:
            w_k, w_qv_t = split_qkv(a_w_qkv[j])
            q_t, k, v_t = _norm_proj_t(xf, g, w_k, w_qv_t, 2, b, ATTN_A_TQ, ATTN_A_TQ)
            bias = _attn_a_bias(a_rel_bias[j], ATTN_A_TQ, ATTN_A_TQ + LEFT)
            o = _attn_a(q_t, k.reshape(b, s, d), v_t, bias)
            xf = _proj_ffn(o.reshape(b * s, d), _Layer(a_w_o, j), xf, g, *ff2)
        elif kind == 1:
            u, v = _gmlp_in(xf, g, _Layer(b_w_in, j), _Layer(b_ln_g, j), _Layer(b_ln_b, j), 2)
            xf = _gmlp_ffn(u, v, _Layer(b_w_s, j), _Layer(b_b_s_t, j), _Layer(b_w_o, j), xf, g, *ff2)
        else:
            lambda_init = 0.8 - 0.6 * math.exp(-0.3 * i)
            w_k, w_qv_t = split_qkv(c_w_qkv[j])
            q_t, k, v_t = _norm_proj_t(xf, g, w_k, w_qv_t, 2, b, ATTN_C_TILE, ATTN_C_TILE)
            o = _attn_c(q_t, k.reshape(b, s, d), v_t, _Layer(c_lambda, j), _Layer(c_subln_g, j),
                        lambda_init)
            xf = _proj_ffn(o.reshape(b * s, d), _Layer(c_w_o, j), xf, g, *ff2)
    return xf.reshape(b, s, d)
```

```python
import functools
import math
from typing import NamedTuple

import jax
import jax.numpy as jnp
import numpy as np
from jax import lax
from jax.experimental import pallas as pl
from jax.experimental.pallas import tpu as pltpu

F32 = jnp.float32
BF16 = jnp.bfloat16

D_MODEL = 1024
D_FF = 2816
EPS = 1e-6
NEG_INF = -1e30
N_MIXERS = 3

CHUNK = 64
LEFT = 8 * CHUNK
HEADS_A = 16
REL_CLIP = 128
GMLP_CHUNK = 128
GMLP_GROUPS = 8
HEADS_C = 8
HEAD_W = 128
LOG2_E = math.log2(math.e)
LOG2_E_BF16 = (1.4453125, -0.00262451171875, 7.063150405883789e-06)

V7X_VMEM_LIMIT_BYTES = 56 * 1024 * 1024
V7X_VREG_BYTES = 8 * 128 * 4


def _pad_rows(dtype):
    return V7X_VREG_BYTES // (128 * jnp.dtype(dtype).itemsize)

ROW_TILE = 512
QKV_ROW_TILE = 1024
FFN_ROW_TILE = 1024
PROJ_FFN_ROW_TILE = 1024
FFN_SUB = 256
ATTN_A_TQ = 128
ATTN_A_HEADS = 16
ATTN_C_TILE = 512
ATTN_C_HEADS = 4
ATTN_C_UNROLL = 2


def _params(n_axes):
    return pltpu.CompilerParams(
        dimension_semantics=("arbitrary",) * n_axes,
        vmem_limit_bytes=V7X_VMEM_LIMIT_BYTES,
    )


class _Layer(NamedTuple):
    stack: jax.Array
    index: int


def _resident(p):
    if isinstance(p, _Layer):
        tail = p.stack.shape[1:]
        return pl.BlockSpec((None,) + tail, lambda *_: (p.index,) + (0,) * len(tail),
                            pipeline_mode=pl.Buffered(1))
    return pl.BlockSpec(p.shape, lambda *_: (0,) * p.ndim, pipeline_mode=pl.Buffered(1))


def _operands(*params):
    return [p.stack if isinstance(p, _Layer) else p for p in params]


def _rms(x, g):
    return x * lax.rsqrt(jnp.mean(x * x, axis=-1, keepdims=True) + EPS) * g


def _mm(a, b):
    return jnp.dot(a, b, preferred_element_type=F32)


def _mm_nt(a, b):
    return lax.dot_general(a, b, (((1,), (1,)), ((), ())), preferred_element_type=F32)


def _ffn_chains(x_ref, g, win_ref, wout_ref, o_ref, g_row):
    chains = [slice(r * FFN_SUB, (r + 1) * FFN_SUB) for r in range(x_ref.shape[0] // FFN_SUB)]

    def up_proj(rows):
        return _mm(_rms(x_ref[rows, :], g[g_row:g_row + 1]).astype(BF16), win_ref[...])

    def down_proj(gu):
        gate = gu[:, :D_FF]
        up = gu[:, D_FF:]
        h = (gate * (1.0 / (1.0 + jnp.exp(-gate))) * up).astype(BF16)
        return _mm(h, wout_ref[...])

    gu_next = up_proj(chains[0])
    y_prev = None
    for r in range(len(chains)):
        gu = gu_next
        if r + 1 < len(chains):
            gu_next = up_proj(chains[r + 1])
        y = down_proj(gu)
        if y_prev is not None:
            o_ref[chains[r - 1], :] = (x_ref[chains[r - 1], :]
                                       + 0.5 * _rms(y_prev, g[g_row + 1:g_row + 2]))
        y_prev = y
    o_ref[chains[-1], :] = x_ref[chains[-1], :] + 0.5 * _rms(y_prev, g[g_row + 1:g_row + 2])


def _ffn_kernel(x_ref, g_ref, win_ref, wout_ref, o_ref, *, g_row):
    _ffn_chains(x_ref, g_ref[...], win_ref, wout_ref, o_ref, g_row)


def _proj_ffn_kernel(a_ref, wo_ref, x_ref, g_ref, win_ref, wout_ref, o_ref, x2_ref, *, g_row):
    g = g_ref[...]
    for r in range(x_ref.shape[0] // FFN_SUB):
        rows = slice(r * FFN_SUB, (r + 1) * FFN_SUB)
        m = _mm(a_ref[rows, :], wo_ref[...])
        x2_ref[rows, :] = x_ref[rows, :] + _rms(m, g[g_row - 1:g_row])
    _ffn_chains(x2_ref, g, win_ref, wout_ref, o_ref, g_row)


def _proj_ffn(a, w_o, x, g, w_in, w_out, g_row):
    t, d = x.shape
    rows = pl.BlockSpec((PROJ_FFN_ROW_TILE, d), lambda i: (i, 0))
    return pl.pallas_call(
        functools.partial(_proj_ffn_kernel, g_row=g_row),
        grid=(t // PROJ_FFN_ROW_TILE,),
        in_specs=[rows, _resident(w_o), rows, _resident(g), _resident(w_in), _resident(w_out)],
        out_specs=rows,
        out_shape=jax.ShapeDtypeStruct((t, d), F32),
        scratch_shapes=[pltpu.VMEM((PROJ_FFN_ROW_TILE, d), F32)],
        compiler_params=_params(1),
        name="proj_ffn",
    )(a, *_operands(w_o), x, *_operands(g, w_in, w_out))


def _ffn(x, g, w_in, w_out, g_row):
    t, d = x.shape
    return pl.pallas_call(
        functools.partial(_ffn_kernel, g_row=g_row),
        grid=(t // FFN_ROW_TILE,),
        in_specs=[
            pl.BlockSpec((FFN_ROW_TILE, d), lambda i: (i, 0)),
            _resident(g),
            _resident(w_in),
            _resident(w_out),
        ],
        out_specs=pl.BlockSpec((FFN_ROW_TILE, d), lambda i: (i, 0)),
        out_shape=jax.ShapeDtypeStruct((t, d), F32),
        compiler_params=_params(1),
        name="ffn",
    )(x, *_operands(g, w_in, w_out))


N_GROUPS = D_MODEL // HEAD_W


def _norm_proj_t_kernel(x_ref, g_ref, wk_ref, wqv_ref, qt_ref, k_ref, vt_ref, *, g_row, tq, tv):
    chain = ROW_TILE
    chains = [slice(r * chain, (r + 1) * chain) for r in range(x_ref.shape[0] // chain)]
    xns = [_rms(x_ref[rows, :], g_ref[g_row:g_row + 1, :]).astype(BF16) for rows in chains]
    for r, (rows, xn) in enumerate(zip(chains, xns)):
        k_ref[rows, :] = _mm(xn, wk_ref[...]).astype(k_ref.dtype)
        qv_t = _mm_nt(wqv_ref[...], xn)
        q_t = (qv_t[:D_MODEL] * ((HEAD_W // 2) ** -0.5 * LOG2_E)).astype(BF16)
        v_t = qv_t[D_MODEL:].astype(BF16)
        for part in range(chain // tq):
            tile = q_t[:, part * tq:(part + 1) * tq].reshape(N_GROUPS, HEAD_W, tq)
            qt_ref[0, :, r * (chain // tq) + part] = tile
        for part in range(chain // tv):
            tile = v_t[:, part * tv:(part + 1) * tv].reshape(N_GROUPS, HEAD_W, tv)
            vt_ref[0, :, r * (chain // tv) + part] = tile


def _norm_proj_t(x, g, w_k, w_qv_t, g_row, batch, tq, tv):
    t, d = x.shape
    s = t // batch
    rows = QKV_ROW_TILE
    per_batch = s // rows

    def tiles(width):
        return pl.BlockSpec((1, N_GROUPS, rows // width, HEAD_W, width),
                            lambda i: (i // per_batch, 0, i % per_batch, 0, 0))

    return pl.pallas_call(
        functools.partial(_norm_proj_t_kernel, g_row=g_row, tq=tq, tv=tv),
        grid=(t // rows,),
        in_specs=[
            pl.BlockSpec((rows, d), lambda i: (i, 0)),
            _resident(g),
            _resident(w_k),
            _resident(w_qv_t),
        ],
        out_specs=[tiles(tq), pl.BlockSpec((rows, d), lambda i: (i, 0)), tiles(tv)],
        out_shape=[
            jax.ShapeDtypeStruct((batch, N_GROUPS, s // tq, HEAD_W, tq), BF16),
            jax.ShapeDtypeStruct((t, d), BF16),
            jax.ShapeDtypeStruct((batch, N_GROUPS, s // tv, HEAD_W, tv), BF16),
        ],
        compiler_params=_params(1),
        name="norm_proj_t",
    )(x, *_operands(g, w_k, w_qv_t))


def _attn_a_bias(rel_bias, tq, win):
    heads = rel_bias.shape[0]
    tbl = rel_bias.astype(F32)
    f = jnp.concatenate([
        jnp.broadcast_to(tbl[:, 2 * REL_CLIP:], (heads, tq + LEFT - REL_CLIP)),
        tbl[:, 2 * REL_CLIP - 1:0:-1],
    ], axis=1)
    p = win + tq
    skew = jnp.tile(jnp.pad(f, ((0, 0), (0, 1))), (1, tq))[:, :tq * (p - 1)]
    bias = skew.reshape(heads, tq, p - 1)[:, :, tq - 1:tq - 1 + win]
    r = np.arange(tq)[:, None] // CHUNK
    j = np.arange(win)[None, :]
    valid = (j >= r * CHUNK) & (j < (r + LEFT // CHUNK + 1) * CHUNK)
    bias = jnp.where(valid, bias * LOG2_E, NEG_INF)
    bias = jnp.pad(bias, ((0, 0), (0, 0), (0, LEFT)), constant_values=NEG_INF)
    bias = bias.reshape(HEADS_A // 2, 2, tq, win + LEFT)
    return jnp.transpose(bias, (0, 3, 1, 2)).reshape(HEADS_A // 2, win + LEFT, 2 * tq)


def _attn_a_kernel(qt_ref, k_ref, vt_ref, bias_ref, o_ref, *, tq, win):
    t = pl.program_id(2)
    wb = jnp.maximum(t - LEFT // tq, 0)
    ws = pl.multiple_of(wb * tq, tq)
    shift = pl.multiple_of(jnp.maximum(LEFT - t * tq, 0), tq)
    first = lax.broadcasted_iota(jnp.int32, (HEAD_W, tq), 0) < HEAD_W // 2
    pairs = ATTN_A_HEADS // 2

    def scores(pp):
        zero = jnp.zeros((HEAD_W, tq), BF16)
        blocks = []
        for p in (2 * pp, 2 * pp + 1):
            qt = qt_ref[0, p, 0]
            row = [zero] * 4
            row[2 * (p % 2)] = jnp.where(first, qt, zero)
            row[2 * (p % 2) + 1] = jnp.where(first, zero, qt)
            blocks.append(jnp.concatenate(row, axis=1))
        qs = jnp.concatenate(blocks, axis=0)
        keys = k_ref[0, pl.ds(ws, win), 2 * pp * HEAD_W:(2 * pp + 2) * HEAD_W]
        bias = jnp.concatenate([bias_ref[2 * pp, pl.ds(shift, win), :],
                                bias_ref[2 * pp + 1, pl.ds(shift, win), :]], axis=1)
        return _mm(keys, qs) + bias

    def attend(p, s):
        m = jnp.max(s, axis=0, keepdims=True)
        e = jnp.exp2(s - m)
        v_t = jnp.concatenate([vt_ref[0, p, wb + c] for c in range(win // tq)], axis=1)
        v_t = jnp.concatenate([v_t, jnp.ones((_pad_rows(BF16), win), BF16)], axis=0)
        o_l = _mm(v_t, e.astype(BF16))
        o_t = o_l[:HEAD_W] * (1.0 / o_l[HEAD_W:HEAD_W + 1])
        o_t = jnp.where(first, o_t[:, :tq], o_t[:, tq:])
        o_ref[0, :, p * HEAD_W:(p + 1) * HEAD_W] = o_t.T.astype(o_ref.dtype)

    s_next = scores(0)
    for pp in range(pairs // 2):
        s_cur = s_next
        if pp + 1 < pairs // 2:
            s_next = scores(pp + 1)
        attend(2 * pp, s_cur[:, :2 * tq])
        attend(2 * pp + 1, s_cur[:, 2 * tq:])


def _attn_a(q_t, k, v_t, bias):
    b, s, _ = k.shape
    tq = ATTN_A_TQ
    win = tq + LEFT
    pairs = ATTN_A_HEADS // 2
    gw = pairs * HEAD_W
    assert bias.shape == (HEADS_A // 2, win + LEFT, 2 * tq)
    return pl.pallas_call(
        functools.partial(_attn_a_kernel, tq=tq, win=win),
        grid=(b, D_MODEL // gw, s // tq),
        in_specs=[
            pl.BlockSpec((1, pairs, 1, HEAD_W, tq), lambda bi, hg, t: (bi, hg, t, 0, 0)),
            pl.BlockSpec((1, s, gw), lambda bi, hg, t: (bi, 0, hg), pipeline_mode=pl.Buffered(1)),
            pl.BlockSpec((1, pairs, s // tq, HEAD_W, tq), lambda bi, hg, t: (bi, hg, 0, 0, 0),
                         pipeline_mode=pl.Buffered(1)),
            pl.BlockSpec((pairs, win + LEFT, 2 * tq), lambda bi, hg, t: (hg, 0, 0),
                         pipeline_mode=pl.Buffered(1)),
        ],
        out_specs=pl.BlockSpec((1, tq, gw), lambda bi, hg, t: (bi, t, hg)),
        out_shape=jax.ShapeDtypeStruct((b, s, D_MODEL), BF16),
        compiler_params=_params(3),
        name="attn_a",
    )(q_t, k, v_t, bias)


def _gmlp_in_kernel(x_ref, g_ref, w_ref, lng_ref, lnb_ref, u_ref, v_ref, *, g_row):
    chains = [slice(r * FFN_SUB, (r + 1) * FFN_SUB) for r in range(x_ref.shape[0] // FFN_SUB)]
    hws = [_mm(_rms(x_ref[rows, :], g_ref[g_row:g_row + 1, :]).astype(BF16), w_ref[...])
           for rows in chains]
    for rows, hw in zip(chains, hws):
        act = 0.5 * hw * (1.0 + lax.erf(hw * math.sqrt(0.5)))
        u_ref[rows, :] = act[:, :D_MODEL].astype(u_ref.dtype)
        v = act[:, D_MODEL:]
        mu = jnp.mean(v, axis=-1, keepdims=True)
        vc = v - mu
        var = jnp.mean(vc * vc, axis=-1, keepdims=True)
        v_ref[rows, :] = (vc * lax.rsqrt(var + EPS) * lng_ref[...] + lnb_ref[...]).astype(v_ref.dtype)


def _gmlp_in(x, g, w, ln_g, ln_b, g_row):
    t, d = x.shape
    out = jax.ShapeDtypeStruct((t, d), BF16)
    row_spec = pl.BlockSpec((ROW_TILE, d), lambda i: (i, 0))
    return pl.pallas_call(
        functools.partial(_gmlp_in_kernel, g_row=g_row),
        grid=(t // ROW_TILE,),
        in_specs=[row_spec, _resident(g), _resident(w),
                  _resident(ln_g), _resident(ln_b)],
        out_specs=[row_spec, row_spec],
        out_shape=[out, out],
        compiler_params=_params(1),
        name="gmlp_in",
    )(x, *_operands(g, w, ln_g, ln_b))


def _gmlp_ffn_kernel(u_ref, v_ref, ws_ref, bs_ref, wo_ref, x_ref, g_ref, win_ref, wout_ref,
                     out_ref, y_ref, x2_ref, *, g_row):
    c = GMLP_CHUNK
    row = lax.broadcasted_iota(jnp.int32, (c, c), 0)
    col = lax.broadcasted_iota(jnp.int32, (c, c), 1)
    causal = row >= col
    for grp in range(GMLP_GROUPS):
        w = ws_ref[grp]
        w = jnp.where(causal, w, jnp.zeros_like(w))
        b = bs_ref[:, grp:grp + 1]
        lanes = slice(grp * c, (grp + 1) * c)
        for blk in range(x_ref.shape[0] // c):
            rows = slice(blk * c, (blk + 1) * c)
            sv = _mm(w, v_ref[rows, lanes]) + b
            y_ref[rows, lanes] = (u_ref[rows, lanes].astype(F32) * sv).astype(y_ref.dtype)
    g = g_ref[...]
    for r in range(x_ref.shape[0] // FFN_SUB):
        rows = slice(r * FFN_SUB, (r + 1) * FFN_SUB)
        m = _mm(y_ref[rows, :], wo_ref[...])
        x2_ref[rows, :] = x_ref[rows, :] + _rms(m, g[g_row - 1:g_row])
    _ffn_chains(x2_ref, g, win_ref, wout_ref, out_ref, g_row)


def _gmlp_ffn(u, v, w_s, b_s_t, w_o, x, g, w_in, w_out, g_row):
    t, d = x.shape
    row_spec = pl.BlockSpec((ROW_TILE, d), lambda i: (i, 0))
    return pl.pallas_call(
        functools.partial(_gmlp_ffn_kernel, g_row=g_row),
        grid=(t // ROW_TILE,),
        in_specs=[row_spec, row_spec, _resident(w_s), _resident(b_s_t), _resident(w_o), row_spec,
                  _resident(g), _resident(w_in), _resident(w_out)],
        out_specs=row_spec,
        out_shape=jax.ShapeDtypeStruct((t, d), F32),
        scratch_shapes=[pltpu.VMEM((ROW_TILE, d), BF16), pltpu.VMEM((ROW_TILE, d), F32)],
        compiler_params=_params(1),
        name="gmlp_ffn",
    )(u, v, *_operands(w_s, b_s_t, w_o), x, *_operands(g, w_in, w_out))


def _attn_c_kernel(lam_ref, subg_ref, qt_ref, k_ref, vt_ref, o_ref,
                   qs_ref, kaug_ref, s_ref, p_ref, diag_ref, smax_ref, m_ref, alpha_ref, acc_ref,
                   *, tile, lambda_init):
    t = pl.program_id(2)
    n = 2 * tile
    lanes = HEAD_W
    group = tile
    n_aug = 2 * len(LOG2_E_BF16)
    ones_rows = jnp.ones((_pad_rows(BF16), tile), BF16)
    first = lax.broadcasted_iota(jnp.int32, (HEAD_W, tile), 0) < HEAD_W // 2
    lam = lam_ref[...]
    lam_full = (jnp.exp(jnp.sum(lam[0:1] * lam[1:2], axis=-1, keepdims=True))
                - jnp.exp(jnp.sum(lam[2:3] * lam[3:4], axis=-1, keepdims=True)) + lambda_init)

    class Head:
        def __init__(self, hh):
            self.hh = hh
            self.cols = slice(hh * HEAD_W, (hh + 1) * HEAD_W)
            h = pl.program_id(1) * ATTN_C_HEADS + hh
            self.inv_slope = jnp.left_shift(1, jnp.full((1, lanes), h + 1, jnp.int32)).astype(F32)

        def build_tables(self):
            hh = self.hh
            key = lax.broadcasted_iota(jnp.int32, (tile, lanes), 0)
            lane = lax.broadcasted_iota(jnp.int32, (tile, lanes), 1)
            lo = (key & 255).astype(F32) / self.inv_slope
            hi = (key & 256).astype(F32) / self.inv_slope
            kaug_ref[hh] = jnp.where(lane < n_aug // 2, lo,
                                     jnp.where(lane < n_aug, hi, 0.0)).astype(BF16)
            row = lax.broadcasted_iota(jnp.int32, (HEAD_W, n), 0)
            consts = jnp.zeros((HEAD_W, n), F32)
            for i, c in enumerate(LOG2_E_BF16 + LOG2_E_BF16):
                consts = jnp.where(row == i, c, consts)
            qs_ref[hh, HEAD_W:, :] = consts.astype(BF16)
            for c in range(tile // lanes):
                qry = lax.broadcasted_iota(jnp.int32, (tile, lanes), 1) + c * lanes
                future = jnp.minimum(2 * (qry - key), 0).astype(F32) / self.inv_slope * LOG2_E
                allowed = jnp.right_shift(key, 6) <= jnp.right_shift(qry, 6)
                diag_ref[hh, c, :tile, :] = jnp.where(allowed, future, NEG_INF)

        def start_tile(self):
            hh = self.hh
            qt = qt_ref[0, hh, 0]
            zero = jnp.zeros_like(qt)
            qs_ref[hh, :HEAD_W, :tile] = jnp.where(first, qt, zero)
            qs_ref[hh, :HEAD_W, tile:] = jnp.where(first, zero, qt)
            m_ref[hh] = jnp.full((1, n), NEG_INF, F32)
            acc_ref[hh] = jnp.zeros(acc_ref.shape[1:], F32)
            p_ref[hh, 1] = jnp.zeros(p_ref.shape[2:], BF16)
            alpha_ref[hh, 1] = jnp.ones((1, n), F32)

        def issue_scores(self, j, slot):
            hh = self.hh
            start = pl.multiple_of(j * tile, tile)
            keys = jnp.concatenate([k_ref[0, pl.ds(start, tile), self.cols], kaug_ref[hh]], axis=1)
            for g in range(n // group):
                gcols = slice(g * group, (g + 1) * group)
                s = _mm(keys, qs_ref[hh, :, gcols])
                for c in range(group // lanes):
                    s_ref[hh, slot, g * (group // lanes) + c, :tile, :] = (
                        s[:, c * lanes:(c + 1) * lanes])
                smax_ref[hh, slot, :, gcols] = jnp.max(s, axis=0, keepdims=True)

        def softmax(self, i, slot, own_block):
            hh = self.hh
            if own_block:
                shift = jnp.zeros((1, lanes), F32)
            else:
                shift = (jnp.full((1, lanes), (i - t) * tile, jnp.int32).astype(F32)
                         / self.inv_slope * sum(LOG2_E_BF16))
            for c in range(n // lanes):
                cols = slice(c * lanes, (c + 1) * lanes)
                if own_block:
                    q_tile = c % (tile // lanes)
                    keys = (q_tile + 1) * lanes
                    sc = s_ref[hh, slot, c, :keys, :] + diag_ref[hh, q_tile, :keys, :]
                    s_ref[hh, slot, c, :keys, :] = sc
                    smax = jnp.max(sc, axis=0, keepdims=True)
                    if keys < tile:
                        p_ref[hh, slot, c, keys:tile, :] = jnp.zeros((tile - keys, lanes), BF16)
                else:
                    keys = tile
                    smax = smax_ref[hh, slot, :, cols]
                m_old = m_ref[hh, :, cols]
                m_new = jnp.maximum(m_old, smax + shift)
                alpha = jnp.exp2(m_old - m_new)
                e = jnp.exp2(s_ref[hh, slot, c, :keys, :] - (m_new - shift))
                m_ref[hh, :, cols] = m_new
                p_ref[hh, slot, c, :keys, :] = e.astype(BF16)
                alpha_ref[hh, slot, :, cols] = alpha

        def values(self, j, slot):
            hh = self.hh
            p = jnp.concatenate([p_ref[hh, slot, c, :tile, :] for c in range(n // lanes)], axis=1)
            v_t = jnp.concatenate([vt_ref[0, hh, j], ones_rows], axis=0)
            acc_ref[hh] = alpha_ref[hh, slot] * acc_ref[hh] + _mm(v_t, p)

        def finish_tile(self):
            hh = self.hh
            o_t = acc_ref[hh, :HEAD_W, :] * (1.0 / acc_ref[hh, HEAD_W:HEAD_W + 1, :])
            a = (o_t[:, :tile] - lam_full * o_t[:, tile:]).T
            o_ref[0, :, self.cols] = (_rms(a, subg_ref[...])
                                      * (1.0 - lambda_init)).astype(o_ref.dtype)

    heads = [Head(hh) for hh in range(ATTN_C_HEADS)]

    @pl.when(t == 0)
    def _():
        for head in heads:
            head.build_tables()

    def step(i, slot, own_block):
        for head in heads:
            head.values(jnp.maximum(i - 1, 0), 1 - slot)
        if not own_block:
            for head in heads:
                head.issue_scores(i + 1, 1 - slot)
        for head in heads:
            head.softmax(i, slot, own_block)

    for head in heads:
        head.start_tile()
    for head in heads:
        head.issue_scores(0, 0)

    def past_blocks(start, count):
        for k in range(count):
            step(start + k, k % 2, False)

    def unrolled_past_blocks(u, carry):
        past_blocks(ATTN_C_UNROLL * u, ATTN_C_UNROLL)
        return carry

    lax.fori_loop(0, t // ATTN_C_UNROLL, unrolled_past_blocks, 0)

    for left in range(ATTN_C_UNROLL):
        @pl.when(t % ATTN_C_UNROLL == left)
        def _(left=left):
            past_blocks(t - left, left)
            step(t, left % 2, True)
            for head in heads:
                head.values(t, left % 2)

    for head in heads:
        head.finish_tile()


def _attn_c(q_t, k, v_t, lam, subln_g, lambda_init):
    b, s, _ = k.shape
    tile = ATTN_C_TILE
    n = 2 * tile
    hp = ATTN_C_HEADS
    return pl.pallas_call(
        functools.partial(_attn_c_kernel, tile=tile, lambda_init=lambda_init),
        grid=(b, HEADS_C // hp, s // tile),
        in_specs=[
            _resident(lam),
            _resident(subln_g),
            pl.BlockSpec((1, hp, 1, HEAD_W, tile), lambda bi, h, t: (bi, h, t, 0, 0)),
            pl.BlockSpec((1, s, hp * HEAD_W), lambda bi, h, t: (bi, 0, h),
                         pipeline_mode=pl.Buffered(1)),
            pl.BlockSpec((1, hp, s // tile, HEAD_W, tile), lambda bi, h, t: (bi, h, 0, 0, 0),
                         pipeline_mode=pl.Buffered(1)),
        ],
        out_specs=pl.BlockSpec((1, tile, hp * HEAD_W), lambda bi, h, t: (bi, t, h)),
        out_shape=jax.ShapeDtypeStruct((b, s, D_MODEL), BF16),
        scratch_shapes=[
            pltpu.VMEM((hp, 2 * HEAD_W, n), BF16),
            pltpu.VMEM((hp, tile, HEAD_W), BF16),
            pltpu.VMEM((hp, 2, n // HEAD_W, tile + _pad_rows(F32), HEAD_W), F32),
            pltpu.VMEM((hp, 2, n // HEAD_W, tile + _pad_rows(BF16), HEAD_W), BF16),
            pltpu.VMEM((hp, tile // HEAD_W, tile + _pad_rows(F32), HEAD_W), F32),
            pltpu.VMEM((hp, 2, 1, n), F32),
            pltpu.VMEM((hp, 1, n), F32),
            pltpu.VMEM((hp, 2, 1, n), F32),
            pltpu.VMEM((hp, HEAD_W + _pad_rows(BF16), n), F32),
        ],
        compiler_params=_params(3),
        name="attn_c",
    )(*_operands(lam, subln_g), q_t, k, v_t)


def kernel(x, norm_g, ff1_w_in, ff1_w_out, ff2_w_in, ff2_w_out, a_w_qkv, a_rel_bias, a_w_o,
           b_w_in, b_ln_g, b_ln_b, b_w_s, b_b_s, b_w_o, c_w_qkv, c_lambda, c_subln_g, c_w_o):
    b, s, d = x.shape
    depth = norm_g.shape[0]
    ff1_w_in, ff1_w_out, ff2_w_in, ff2_w_out, a_w_o, b_w_in, b_w_s, b_w_o, c_w_o = (
        w.astype(BF16) for w in
        (ff1_w_in, ff1_w_out, ff2_w_in, ff2_w_out, a_w_o, b_w_in, b_w_s, b_w_o, c_w_o))

    def split_qkv(w):
        w = w.astype(BF16)
        return w[:, d:2 * d], jnp.concatenate([w[:, :d], w[:, 2 * d:]], axis=1).T

    b_ln_g, b_ln_b, c_subln_g = b_ln_g[:, None], b_ln_b[:, None], c_subln_g[:, None]
    b_b_s_t = jnp.swapaxes(b_b_s, 1, 2)
    xf = x.reshape(b * s, d)
    for i in range(depth):
        g = _Layer(norm_g, i)
        xf = _ffn(xf, g, _Layer(ff1_w_in, i), _Layer(ff1_w_out, i), 0)
        ff2 = (_Layer(ff2_w_in, i), _Layer(ff2_w_out, i), 4)
        kind, j = i % N_MIXERS, i // N_MIXERS
        if kind == 0:
            w_k, w_qv_t = split_qkv(a_w_qkv[j])
            q_t, k, v_t = _norm_proj_t(xf, g, w_k, w_qv_t, 2, b, ATTN_A_TQ, ATTN_A_TQ)
            bias = _attn_a_bias(a_rel_bias[j], ATTN_A_TQ, ATTN_A_TQ + LEFT)
            o = _attn_a(q_t, k.reshape(b, s, d), v_t, bias)
            xf = _proj_ffn(o.reshape(b * s, d), _Layer(a_w_o, j), xf, g, *ff2)
        elif kind == 1:
            u, v = _gmlp_in(xf, g, _Layer(b_w_in, j), _Layer(b_ln_g, j), _Layer(b_ln_b, j), 2)
            xf = _gmlp_ffn(u, v, _Layer(b_w_s, j), _Layer(b_b_s_t, j), _Layer(b_w_o, j), xf, g, *ff2)
        else:
            lambda_init = 0.8 - 0.6 * math.exp(-0.3 * i)
            w_k, w_qv_t = split_qkv(c_w_qkv[j])
            q_t, k, v_t = _norm_proj_t(xf, g, w_k, w_qv_t, 2, b, ATTN_C_TILE, ATTN_C_TILE)
            o = _attn_c(q_t, k.reshape(b, s, d), v_t, _Layer(c_lambda, j), _Layer(c_subln_g, j),
                        lambda_init)
            xf = _proj_ffn(o.reshape(b * s, d), _Layer(c_w_o, j), xf, g, *ff2)
    return xf.reshape(b, s, d)
```
